```python
import jax, jax.numpy as jnp
from jax import lax
import numpy as np

D_MODEL = 2048
BATCH = 4
SEQ = 4096
DEPTH = 1

POOL_WIDTH = D_MODEL // 2
POOL_WINDOWS = (2, 4, 8, 16)
N_POOL_GROUPS = len(POOL_WINDOWS)
POOL_GROUP = POOL_WIDTH // N_POOL_GROUPS
CONV_WIDTH = D_MODEL // 2
CONV_KERNEL = 31
D_FF = ((8 * D_MODEL // 3 + 255) // 256) * 256
N_BRANCHES = 2
IN_WIDTH = POOL_WIDTH + 2 * CONV_WIDTH + N_BRANCHES * D_MODEL
FFN_RESIDUAL_WEIGHT = 0.5
EPS = 1e-6

kernel_name = "gated_pool_conformer_macaron_block"


def rmsnorm(x, g):
    xf = x.astype(jnp.float32)
    y = xf * lax.rsqrt(jnp.mean(xf * xf, axis=-1, keepdims=True) + EPS)
    return (y * g.astype(jnp.float32)).astype(x.dtype)


def layernorm(x, g, b):
    xf = x.astype(jnp.float32)
    mu = jnp.mean(xf, axis=-1, keepdims=True)
    var = jnp.mean(jnp.square(xf - mu), axis=-1, keepdims=True)
    y = (xf - mu) * lax.rsqrt(var + EPS)
    return (y * g.astype(jnp.float32) + b.astype(jnp.float32)).astype(x.dtype)


def swiglu_ffn(h, w_in, w_out):
    gate, up = jnp.split(h @ w_in, 2, axis=-1)
    return (jax.nn.silu(gate) * up) @ w_out


def causal_multiscale_pool(u):
    S = u.shape[1]
    uf = u.astype(jnp.float32)
    cs = jnp.pad(jnp.cumsum(uf, axis=1), ((0, 0), (1, 0), (0, 0)))
    t = jnp.arange(S)
    outs = []
    for g, w in enumerate(POOL_WINDOWS):
        sl = slice(g * POOL_GROUP, (g + 1) * POOL_GROUP)
        lo = jnp.maximum(t + 1 - w, 0)
        window_sum = cs[:, 1:, sl] - jnp.take(cs[:, :, sl], lo, axis=1)
        count = jnp.minimum(t + 1, w).astype(jnp.float32)[None, :, None]
        outs.append(window_sum / count - uf[..., sl])
    return jnp.stack(outs, axis=2).astype(u.dtype)


def causal_depthwise_conv(u, k, b):
    C = u.shape[-1]
    y = lax.conv_general_dilated(
        u, k.reshape(CONV_KERNEL, 1, C).astype(u.dtype),
        window_strides=(1,), padding=[(CONV_KERNEL - 1, 0)],
        dimension_numbers=("NWC", "WIO", "NWC"), feature_group_count=C)
    return y + b


def mixer_block(h, w_in, b_in, pool_w_grp, pool_scale, pool_w_proj,
                conv_dw, conv_b, conv_ln_g, conv_ln_b, conv_w_proj, w_out):
    B, S, _ = h.shape
    z = h @ w_in + b_in
    u_pool = z[..., :POOL_WIDTH]
    u_conv = z[..., POOL_WIDTH:POOL_WIDTH + 2 * CONV_WIDTH]
    g_logits = z[..., POOL_WIDTH + 2 * CONV_WIDTH:]
    pooled = causal_multiscale_pool(u_pool)
    mixed = jnp.einsum("bsgc,gcd->bsgd", pooled, pool_w_grp).reshape(B, S, POOL_WIDTH)
    a = (mixed * pool_scale) @ pool_w_proj
    glu = u_conv[..., :CONV_WIDTH] * jax.nn.sigmoid(u_conv[..., CONV_WIDTH:])
    c = causal_depthwise_conv(glu, conv_dw, conv_b)
    c = jax.nn.silu(layernorm(c, conv_ln_g, conv_ln_b))
    bb = c @ conv_w_proj
    g_a, g_b = jnp.split(g_logits, N_BRANCHES, axis=-1)
    merged = jax.nn.sigmoid(g_a) * a + jax.nn.sigmoid(g_b) * bb
    return merged @ w_out


def setup_inputs(seed: int = 0) -> dict:
    key = jax.random.key(seed)
    ks = jax.random.split(key, 24)
    f32 = jnp.float32

    def nrm(k, shape, fan_in, scale=1.0):
        return (jax.random.normal(k, shape, f32) * (scale * fan_in ** -0.5)).astype(f32)

    def gain(k, shape):
        return (1.0 + 0.02 * jax.random.normal(k, shape, f32)).astype(f32)

    L = DEPTH
    return {
        "x": jax.random.normal(ks[0], (BATCH, SEQ, D_MODEL), f32),
        "ffn1_norm": gain(ks[1], (L, D_MODEL)),
        "ffn1_w_in": nrm(ks[2], (L, D_MODEL, 2 * D_FF), D_MODEL),
        "ffn1_w_out": nrm(ks[3], (L, D_FF, D_MODEL), D_FF),
        "mix_norm": gain(ks[4], (L, D_MODEL)),
        "w_in": nrm(ks[5], (L, D_MODEL, IN_WIDTH), D_MODEL),
        "b_in": (0.02 * jax.random.normal(ks[6], (L, IN_WIDTH), f32)).astype(f32),
        "pool_w_grp": nrm(ks[7], (L, N_POOL_GROUPS, POOL_GROUP, POOL_GROUP), POOL_GROUP),
        "pool_scale": gain(ks[8], (L, POOL_WIDTH)),
        "pool_w_proj": nrm(ks[9], (L, POOL_WIDTH, D_MODEL), POOL_WIDTH),
        "conv_dw": nrm(ks[10], (L, CONV_KERNEL, CONV_WIDTH), CONV_KERNEL),
        "conv_b": (0.02 * jax.random.normal(ks[11], (L, CONV_WIDTH), f32)).astype(f32),
        "conv_ln_g": gain(ks[12], (L, CONV_WIDTH)),
        "conv_ln_b": (0.02 * jax.random.normal(ks[13], (L, CONV_WIDTH), f32)).astype(f32),
        "conv_w_proj": nrm(ks[14], (L, CONV_WIDTH, D_MODEL), CONV_WIDTH),
        "w_out": nrm(ks[15], (L, D_MODEL, D_MODEL), D_MODEL),
        "ffn2_norm": gain(ks[16], (L, D_MODEL)),
        "ffn2_w_in": nrm(ks[17], (L, D_MODEL, 2 * D_FF), D_MODEL),
        "ffn2_w_out": nrm(ks[18], (L, D_FF, D_MODEL), D_FF),
        "final_norm": gain(ks[19], (D_MODEL,)),
    }


def reference(x, ffn1_norm, ffn1_w_in, ffn1_w_out, mix_norm, w_in, b_in,
              pool_w_grp, pool_scale, pool_w_proj, conv_dw, conv_b, conv_ln_g, conv_ln_b,
              conv_w_proj, w_out, ffn2_norm, ffn2_w_in, ffn2_w_out, final_norm):
    for l in range(DEPTH):
        x = x + FFN_RESIDUAL_WEIGHT * swiglu_ffn(rmsnorm(x, ffn1_norm[l]), ffn1_w_in[l], ffn1_w_out[l])
        x = x + mixer_block(rmsnorm(x, mix_norm[l]), w_in[l], b_in[l],
                            pool_w_grp[l], pool_scale[l], pool_w_proj[l],
                            conv_dw[l], conv_b[l], conv_ln_g[l], conv_ln_b[l],
                            conv_w_proj[l], w_out[l])
        x = x + FFN_RESIDUAL_WEIGHT * swiglu_ffn(rmsnorm(x, ffn2_norm[l]), ffn2_w_in[l], ffn2_w_out[l])
    return rmsnorm(x, final_norm)
```

```python
import functools

import jax
import jax.numpy as jnp
from jax import lax
from jax.experimental import pallas as pl
from jax.experimental.pallas import tpu as pltpu

EPS = 1e-6
FFN_RESIDUAL_WEIGHT = 0.5
POOL_WINDOWS = (2, 4, 8, 16)

HALO = 32
VMEM_LIMIT_BYTES = 56 * 1024 * 1024

_F32 = jnp.float32
_BF16 = jnp.bfloat16


def _rmsnorm(x, g):
    ms = jnp.mean(x * x, axis=-1, keepdims=True)
    return (x * lax.rsqrt(ms + EPS)) * g


def _dot(a, b):
    return jnp.dot(a, b, preferred_element_type=_F32)


def _ffn_kernel(x_ref, g_ref, wg_ref, wu_ref, wo_ref, *rest, row_chunk, col_chunk, final_norm):
    if final_norm:
        fg_ref, o_ref, h_ref = rest
    else:
        o_ref, h_ref = rest
    j = pl.program_id(1)
    tm, d = x_ref.shape
    n_row_chunks = tm // row_chunk

    @pl.when(j == 0)
    def _():
        def body(r, carry):
            r0 = pl.multiple_of(r * row_chunk, row_chunk)
            x = x_ref[pl.ds(r0, row_chunk), :]
            h_ref[pl.ds(r0, row_chunk), :] = _rmsnorm(x, g_ref[...]).astype(_BF16)
            o_ref[pl.ds(r0, row_chunk), :] = jnp.zeros((row_chunk, d), _F32)
            return carry

        lax.fori_loop(0, n_row_chunks, body, 0)

    h = h_ref[...]
    gate = _dot(h, wg_ref[...])
    up = _dot(h, wu_ref[...])
    act = ((gate * jax.nn.sigmoid(gate)) * up).astype(_BF16)
    for n in range(d // col_chunk):
        cs = slice(n * col_chunk, (n + 1) * col_chunk)
        o_ref[:, cs] += _dot(act, wo_ref[:, cs])

    @pl.when(j == pl.num_programs(1) - 1)
    def _():
        def body(r, carry):
            r0 = pl.multiple_of(r * row_chunk, row_chunk)
            rows = pl.ds(r0, row_chunk)
            y = x_ref[rows, :] + FFN_RESIDUAL_WEIGHT * o_ref[rows, :]
            if final_norm:
                y = _rmsnorm(y, fg_ref[...])
            o_ref[rows, :] = y
            return carry

        lax.fori_loop(0, n_row_chunks, body, 0)


def _ffn(x, g, w_in, w_out, final_g=None, *, tm=512, tf=512):
    m, d = x.shape
    f = w_out.shape[0]
    n_f = f // tf
    final_norm = final_g is not None
    in_specs = [
        pl.BlockSpec((tm, d), lambda i, j: (i, 0)),
        pl.BlockSpec((1, d), lambda i, j: (0, 0)),
        pl.BlockSpec((d, tf), lambda i, j: (0, j)),
        pl.BlockSpec((d, tf), lambda i, j: (0, j + n_f)),
        pl.BlockSpec((tf, d), lambda i, j: (j, 0)),
    ]
    args = [x, g, w_in, w_in, w_out]
    if final_norm:
        in_specs.append(pl.BlockSpec((1, d), lambda i, j: (0, 0)))
        args.append(final_g)
    return pl.pallas_call(
        functools.partial(_ffn_kernel, row_chunk=64, col_chunk=512, final_norm=final_norm),
        grid=(m // tm, n_f),
        in_specs=in_specs,
        out_specs=pl.BlockSpec((tm, d), lambda i, j: (i, 0)),
        out_shape=jax.ShapeDtypeStruct((m, d), _F32),
        scratch_shapes=[pltpu.VMEM((tm, d), _BF16)],
        compiler_params=pltpu.CompilerParams(
            dimension_semantics=("parallel", "arbitrary"), vmem_limit_bytes=VMEM_LIMIT_BYTES),
        name="ffn_final" if final_norm else "ffn",
    )(*args)


def _mix_in_kernel(x_ref, halo_ref, g_ref, w_ref, b_ref, wgrp_ref, pscale_ref, dw_ref, cb_ref,
                   lng_ref, lnb_ref, mixed_ref, c_ref, h_ref, u_ref, s_ref, y_ref,
                   *, tiles_per_seq, row_block):
    i = pl.program_id(0)
    tm = x_ref.shape[0]
    pool_width = mixed_ref.shape[1]
    conv_width = c_ref.shape[1]
    n_groups = len(POOL_WINDOWS)
    group = pool_width // n_groups
    n_taps = dw_ref.shape[0]
    tile_in_seq = i % tiles_per_seq
    first_in_seq = tile_in_seq == 0

    h_ref[0:HALO, :] = _rmsnorm(halo_ref[...], g_ref[...]).astype(_BF16)
    h_ref[HALO:HALO + tm, :] = _rmsnorm(x_ref[...], g_ref[...]).astype(_BF16)
    h = h_ref[...]

    u_ref[...] = _dot(h, w_ref[:, 0:pool_width]) + b_ref[:, 0:pool_width]

    @pl.when(first_in_seq)
    def _():
        u_ref[0:HALO, :] = jnp.zeros((HALO, pool_width), _F32)

    pos = tile_in_seq * tm + lax.broadcasted_iota(jnp.int32, (tm, 1), 0)
    for gi, window in enumerate(POOL_WINDOWS):
        cs = slice(gi * group, (gi + 1) * group)
        src, shift, lo, level = u_ref, 1, 8, 0
        while shift < window:
            dst = (s_ref, y_ref)[level % 2]
            dst[lo:, cs] = src[lo:, cs] + src[lo - shift:HALO + tm - shift, cs]
            src, shift, lo, level = dst, shift * 2, lo + 8, level + 1
        inv_count = 1.0 / jnp.minimum(pos + 1, window).astype(_F32)
        pooled = src[HALO:, cs] * inv_count - u_ref[HALO:, cs]
        mixed = _dot(pooled.astype(_BF16), wgrp_ref[gi])
        mixed_ref[:, cs] = (mixed * pscale_ref[:, cs]).astype(_BF16)

    za = _dot(h, w_ref[:, pool_width:pool_width + conv_width]) \
        + b_ref[:, pool_width:pool_width + conv_width]
    zg = _dot(h, w_ref[:, pool_width + conv_width:]) + b_ref[:, pool_width + conv_width:]
    u_ref[...] = za * jax.nn.sigmoid(zg)

    @pl.when(first_in_seq)
    def _():
        u_ref[0:HALO, :] = jnp.zeros((HALO, conv_width), _F32)

    lane_block = 256

    conv_rows = 64
    for r0 in range(0, tm, conv_rows):
        for cb in range(conv_width // lane_block):
            cs = slice(cb * lane_block, (cb + 1) * lane_block)
            acc = jnp.broadcast_to(cb_ref[:, cs], (conv_rows, lane_block))
            for t in range(n_taps):
                lo = r0 + HALO - (n_taps - 1) + t
                acc = acc + dw_ref[t:t + 1, cs] * u_ref[lo:lo + conv_rows, cs]
            y_ref[r0:r0 + conv_rows, cs] = acc

    def ln_rows(r, carry):
        r0 = pl.multiple_of(r * row_block, row_block)
        rows = pl.ds(r0, row_block)
        y = y_ref[rows, :]
        mu = jnp.mean(y, axis=-1, keepdims=True)
        yc = y - mu
        var = jnp.mean(yc * yc, axis=-1, keepdims=True)
        z = (yc * lax.rsqrt(var + EPS)) * lng_ref[...] + lnb_ref[...]
        c_ref[rows, :] = (z * jax.nn.sigmoid(z)).astype(_BF16)
        return carry

    lax.fori_loop(0, tm // row_block, ln_rows, 0)


def _mix_in(x, g, w_pc, b_pc, w_grp, pool_scale, conv_dw, conv_b, ln_g, ln_b, *, seq, tm=512):
    m, d = x.shape
    n_groups, group, _ = w_grp.shape
    pool_width = n_groups * group
    n_taps, conv_width = conv_dw.shape
    assert pool_width == conv_width and n_taps - 1 <= HALO and max(POOL_WINDOWS) <= HALO
    assert seq % tm == 0 and tm % HALO == 0
    halo_blocks = tm // HALO

    def const(shape):
        return pl.BlockSpec(shape, lambda i: (0,) * len(shape), pipeline_mode=pl.Buffered(1))

    return pl.pallas_call(
        functools.partial(_mix_in_kernel, tiles_per_seq=seq // tm, row_block=32),
        grid=(m // tm,),
        in_specs=[
            pl.BlockSpec((tm, d), lambda i: (i, 0)),
            pl.BlockSpec((HALO, d), lambda i: (jnp.maximum(i * halo_blocks - 1, 0), 0)),
            const((1, d)),
            const(w_pc.shape),
            const(b_pc.shape),
            const(w_grp.shape),
            const((1, pool_width)),
            const(conv_dw.shape),
            const((1, conv_width)),
            const((1, conv_width)),
            const((1, conv_width)),
        ],
        out_specs=[
            pl.BlockSpec((tm, pool_width), lambda i: (i, 0)),
            pl.BlockSpec((tm, conv_width), lambda i: (i, 0)),
        ],
        out_shape=[
            jax.ShapeDtypeStruct((m, pool_width), _BF16),
            jax.ShapeDtypeStruct((m, conv_width), _BF16),
        ],
        scratch_shapes=[
            pltpu.VMEM((HALO + tm, d), _BF16),
            pltpu.VMEM((HALO + tm, pool_width), _F32),
            pltpu.VMEM((HALO + tm, pool_width), _F32),
            pltpu.VMEM((HALO + tm, conv_width), _F32),
        ],
        compiler_params=pltpu.CompilerParams(
            dimension_semantics=("parallel",), vmem_limit_bytes=VMEM_LIMIT_BYTES),
        name="mix_in",
    )(x, x, g, w_pc, b_pc, w_grp, pool_scale, conv_dw, conv_b, ln_g, ln_b)


def _mix_out_kernel(x_ref, g_ref, wgate_ref, bgate_ref, mixed_ref, c_ref, pp_ref, cp_ref, wo_ref,
                    o_ref, m_ref, *, col_chunk):
    d = x_ref.shape[1]
    h = _rmsnorm(x_ref[...], g_ref[...]).astype(_BF16)
    mixed = mixed_ref[...]
    c = c_ref[...]
    for n in range(d // col_chunk):
        ca = slice(n * col_chunk, (n + 1) * col_chunk)
        cb = slice(d + n * col_chunk, d + (n + 1) * col_chunk)
        gate_a = jax.nn.sigmoid(_dot(h, wgate_ref[:, ca]) + bgate_ref[:, ca])
        gate_b = jax.nn.sigmoid(_dot(h, wgate_ref[:, cb]) + bgate_ref[:, cb])
        a = _dot(mixed, pp_ref[:, ca])
        bb = _dot(c, cp_ref[:, ca])
        m_ref[:, ca] = (gate_a * a + gate_b * bb).astype(_BF16)
    o_ref[...] = x_ref[...] + _dot(m_ref[...], wo_ref[...])


def _mix_out(x, g, w_gate, b_gate, mixed, c, pool_w_proj, conv_w_proj, w_out, *, tm=256):
    m, d = x.shape
    width = mixed.shape[1]

    def resident(shape):
        return pl.BlockSpec(shape, lambda i: (0,) * len(shape), pipeline_mode=pl.Buffered(1))

    return pl.pallas_call(
        functools.partial(_mix_out_kernel, col_chunk=512),
        grid=(m // tm,),
        in_specs=[
            pl.BlockSpec((tm, d), lambda i: (i, 0)),
            resident((1, d)),
            resident(w_gate.shape),
            resident(b_gate.shape),
            pl.BlockSpec((tm, width), lambda i: (i, 0)),
            pl.BlockSpec((tm, width), lambda i: (i, 0)),
            resident(pool_w_proj.shape),
            resident(conv_w_proj.shape),
            resident(w_out.shape),
        ],
        out_specs=pl.BlockSpec((tm, d), lambda i: (i, 0)),
        out_shape=jax.ShapeDtypeStruct((m, d), _F32),
        scratch_shapes=[pltpu.VMEM((tm, d), _BF16)],
        compiler_params=pltpu.CompilerParams(
            dimension_semantics=("parallel",), vmem_limit_bytes=VMEM_LIMIT_BYTES),
        name="mix_out",
    )(x, g, w_gate, b_gate, mixed, c, pool_w_proj, conv_w_proj, w_out)


def kernel(x, ffn1_norm, ffn1_w_in, ffn1_w_out, mix_norm, w_in, b_in, pool_w_grp, pool_scale,
           pool_w_proj, conv_dw, conv_b, conv_ln_g, conv_ln_b, conv_w_proj, w_out, ffn2_norm,
           ffn2_w_in, ffn2_w_out, final_norm):
    batch, seq, d = x.shape
    depth = ffn1_norm.shape[0]
    pool_width = pool_w_proj.shape[1]
    conv_width = conv_w_proj.shape[1]
    n_pc = pool_width + 2 * conv_width

    def row(v):
        return v.reshape(1, -1).astype(_F32)

    xf = x.reshape(batch * seq, d)
    for l in range(depth):
        xf = _ffn(xf, row(ffn1_norm[l]), ffn1_w_in[l].astype(_BF16), ffn1_w_out[l].astype(_BF16))
        mixed, c = _mix_in(
            xf, row(mix_norm[l]), w_in[l][:, :n_pc].astype(_BF16), row(b_in[l][:n_pc]),
            pool_w_grp[l].astype(_BF16), row(pool_scale[l]), conv_dw[l], row(conv_b[l]),
            row(conv_ln_g[l]), row(conv_ln_b[l]), seq=seq)
        xf = _mix_out(
            xf, row(mix_norm[l]), w_in[l][:, n_pc:].astype(_BF16), row(b_in[l][n_pc:]), mixed, c,
            pool_w_proj[l].astype(_BF16), conv_w_proj[l].astype(_BF16), w_out[l].astype(_BF16))
        last = l == depth - 1
        xf = _ffn(xf, row(ffn2_norm[l]), ffn2_w_in[l].astype(_BF16), ffn2_w_out[l].astype(_BF16),
                  row(final_norm) if last else None)
    if depth == 0:
        raise ValueError("depth must be positive")
    return xf.reshape(batch, seq, d)
```

```python
import functools

import jax
import jax.numpy as jnp
from jax import lax
from jax.experimental import pallas as pl
from jax.experimental.pallas import tpu as pltpu

EPS = 1e-6
FFN_RESIDUAL_WEIGHT = 0.5
POOL_WINDOWS = (2, 4, 8, 16)

HALO = 32
SUBLANES = 8
LANES = 128
ROW_STRIDE = 4
VMEM_LIMIT_BYTES = 56 * 1024 * 1024

_F32 = jnp.float32
_BF16 = jnp.bfloat16


def _rmsnorm(x, g):
    ms = jnp.mean(x * x, axis=-1, keepdims=True)
    return (x * lax.rsqrt(ms + EPS)) * g


def _dot(a, b):
    return jnp.dot(a, b, preferred_element_type=_F32)


def _ffn_kernel(x_ref, g_ref, wg_ref, wu_ref, wo_ref, *rest, row_chunk, col_chunk, final_norm):
    if final_norm:
        fg_ref, o_ref, h_ref = rest
    else:
        o_ref, h_ref = rest
    j = pl.program_id(1)
    tm, d = x_ref.shape
    n_row_chunks = tm // row_chunk

    @pl.when(j == 0)
    def _():
        def body(r, carry):
            r0 = pl.multiple_of(r * row_chunk, row_chunk)
            x = x_ref[pl.ds(r0, row_chunk), :]
            h_ref[pl.ds(r0, row_chunk), :] = _rmsnorm(x, g_ref[...]).astype(_BF16)
            o_ref[pl.ds(r0, row_chunk), :] = jnp.zeros((row_chunk, d), _F32)
            return carry

        lax.fori_loop(0, n_row_chunks, body, 0)

    h = h_ref[...]
    gate = _dot(h, wg_ref[...])
    up = _dot(h, wu_ref[...])
    act = ((gate * jax.nn.sigmoid(gate)) * up).astype(_BF16)
    for n in range(d // col_chunk):
        cs = slice(n * col_chunk, (n + 1) * col_chunk)
        o_ref[:, cs] += _dot(act, wo_ref[:, cs])

    @pl.when(j == pl.num_programs(1) - 1)
    def _():
        def body(r, carry):
            r0 = pl.multiple_of(r * row_chunk, row_chunk)
            rows = pl.ds(r0, row_chunk)
            y = x_ref[rows, :] + FFN_RESIDUAL_WEIGHT * o_ref[rows, :]
            if final_norm:
                y = _rmsnorm(y, fg_ref[...])
            o_ref[rows, :] = y
            return carry

        lax.fori_loop(0, n_row_chunks, body, 0)


def _ffn(x, g, w_in, w_out, final_g=None, *, tm=1024, tf=512):
    m, d = x.shape
    f = w_out.shape[0]
    n_f = f // tf
    final_norm = final_g is not None
    in_specs = [
        pl.BlockSpec((tm, d), lambda i, j: (i, 0)),
        pl.BlockSpec((1, d), lambda i, j: (0, 0)),
        pl.BlockSpec((d, tf), lambda i, j: (0, j)),
        pl.BlockSpec((d, tf), lambda i, j: (0, j + n_f)),
        pl.BlockSpec((tf, d), lambda i, j: (j, 0)),
    ]
    args = [x, g, w_in, w_in, w_out]
    if final_norm:
        in_specs.append(pl.BlockSpec((1, d), lambda i, j: (0, 0)))
        args.append(final_g)
    return pl.pallas_call(
        functools.partial(_ffn_kernel, row_chunk=64, col_chunk=512, final_norm=final_norm),
        grid=(m // tm, n_f),
        in_specs=in_specs,
        out_specs=pl.BlockSpec((tm, d), lambda i, j: (i, 0)),
        out_shape=jax.ShapeDtypeStruct((m, d), _F32),
        scratch_shapes=[pltpu.VMEM((tm, d), _BF16)],
        compiler_params=pltpu.CompilerParams(
            dimension_semantics=("parallel", "arbitrary"), vmem_limit_bytes=VMEM_LIMIT_BYTES),
        name="ffn_final" if final_norm else "ffn",
    )(*args)


def _mix_in_kernel(x_ref, halo_ref, g_ref, w_ref, b_ref, wgrp_ref, pscale_ref, dw_ref, cb_ref,
                   lng_ref, lnb_ref, mixed_ref, c_ref, h_ref, u_ref, s_ref, y_ref, glu_ref, conv_ref,
                   *, tiles_per_seq, row_block):
    i = pl.program_id(0)
    tm = x_ref.shape[0]
    pool_width = mixed_ref.shape[1]
    conv_width = c_ref.shape[1]
    n_groups = len(POOL_WINDOWS)
    group = pool_width // n_groups
    n_taps = dw_ref.shape[0]
    tile_in_seq = i % tiles_per_seq
    first_in_seq = tile_in_seq == 0

    h_ref[0:HALO, :] = _rmsnorm(halo_ref[...], g_ref[...]).astype(_BF16)
    h_ref[HALO:HALO + tm, :] = _rmsnorm(x_ref[...], g_ref[...]).astype(_BF16)
    h = h_ref[...]

    u_ref[...] = _dot(h, w_ref[:, 0:pool_width]) + b_ref[:, 0:pool_width]

    @pl.when(first_in_seq)
    def _():
        u_ref[0:HALO, :] = jnp.zeros((HALO, pool_width), _F32)

    pos = tile_in_seq * tm + lax.broadcasted_iota(jnp.int32, (tm, 1), 0)
    for gi, window in enumerate(POOL_WINDOWS):
        cs = slice(gi * group, (gi + 1) * group)
        src, shift, lo, level = u_ref, 1, 8, 0
        while shift < window:
            dst = (s_ref, y_ref)[level % 2]
            dst[lo:, cs] = src[lo:, cs] + src[lo - shift:HALO + tm - shift, cs]
            src, shift, lo, level = dst, shift * 2, lo + 8, level + 1
        inv_count = 1.0 / jnp.minimum(pos + 1, window).astype(_F32)
        pooled = src[HALO:, cs] * inv_count - u_ref[HALO:, cs]
        mixed = _dot(pooled.astype(_BF16), wgrp_ref[gi])
        mixed_ref[:, cs] = (mixed * pscale_ref[:, cs]).astype(_BF16)

    za = _dot(h, w_ref[:, pool_width:pool_width + conv_width]) \
        + b_ref[:, pool_width:pool_width + conv_width]
    zg = _dot(h, w_ref[:, pool_width + conv_width:]) + b_ref[:, pool_width + conv_width:]
    n_lane_blocks = conv_width // LANES
    for cb in range(n_lane_blocks):
        cs = slice(cb * LANES, (cb + 1) * LANES)
        glu_ref[cb] = za[:, cs] * jax.nn.sigmoid(zg[:, cs])

    @pl.when(first_in_seq)
    def _():
        glu_ref[:, 0:HALO, :] = jnp.zeros((n_lane_blocks, HALO, LANES), _F32)

    rows_per_group = 8 * SUBLANES * ROW_STRIDE
    for cb in range(n_lane_blocks):
        bias = cb_ref[pl.ds(cb, SUBLANES, stride=0), :]
        for r0 in range(0, tm, rows_per_group):
            starts = [r0 + blk * SUBLANES * ROW_STRIDE + ph
                      for blk in range(rows_per_group // (SUBLANES * ROW_STRIDE))
                      for ph in range(ROW_STRIDE)]
            accs = [bias for _ in starts]
            for t in range(n_taps):
                w = dw_ref[t, pl.ds(cb, SUBLANES, stride=0), :]
                off = HALO - (n_taps - 1) + t
                accs = [acc + w * glu_ref[cb, pl.ds(s + off, SUBLANES, stride=ROW_STRIDE), :]
                        for acc, s in zip(accs, starts)]
            for acc, s in zip(accs, starts):
                conv_ref[cb, pl.ds(s, SUBLANES, stride=ROW_STRIDE), :] = acc

    def ln_rows(r, carry):
        r0 = pl.multiple_of(r * row_block, row_block)
        rows = pl.ds(r0, row_block)
        y = [conv_ref[cb, rows, :] for cb in range(n_lane_blocks)]
        mu = jnp.sum(sum(y), axis=-1, keepdims=True) * (1.0 / conv_width)
        yc = [v - mu for v in y]
        var = jnp.sum(sum(v * v for v in yc), axis=-1, keepdims=True) * (1.0 / conv_width)
        inv = lax.rsqrt(var + EPS)
        for cb in range(n_lane_blocks):
            cs = slice(cb * LANES, (cb + 1) * LANES)
            z = (yc[cb] * inv) * lng_ref[:, cs] + lnb_ref[:, cs]
            c_ref[rows, cs] = (z * jax.nn.sigmoid(z)).astype(_BF16)
        return carry

    lax.fori_loop(0, tm // row_block, ln_rows, 0)


def _mix_in(x, g, w_pc, b_pc, w_grp, pool_scale, conv_dw, conv_b, ln_g, ln_b, *, seq, tm=512):
    m, d = x.shape
    n_groups, group, _ = w_grp.shape
    pool_width = n_groups * group
    n_taps, conv_width = conv_dw.shape
    assert pool_width == conv_width and n_taps - 1 <= HALO and max(POOL_WINDOWS) <= HALO
    assert seq % tm == 0 and tm % HALO == 0
    halo_blocks = tm // HALO
    n_lane_blocks = conv_width // LANES

    def const(shape):
        return pl.BlockSpec(shape, lambda i: (0,) * len(shape), pipeline_mode=pl.Buffered(1))

    return pl.pallas_call(
        functools.partial(_mix_in_kernel, tiles_per_seq=seq // tm, row_block=32),
        grid=(m // tm,),
        in_specs=[
            pl.BlockSpec((tm, d), lambda i: (i, 0)),
            pl.BlockSpec((HALO, d), lambda i: (jnp.maximum(i * halo_blocks - 1, 0), 0)),
            const((1, d)),
            const(w_pc.shape),
            const(b_pc.shape),
            const(w_grp.shape),
            const((1, pool_width)),
            const((n_taps, n_lane_blocks, LANES)),
            const((n_lane_blocks, LANES)),
            const((1, conv_width)),
            const((1, conv_width)),
        ],
        out_specs=[
            pl.BlockSpec((tm, pool_width), lambda i: (i, 0)),
            pl.BlockSpec((tm, conv_width), lambda i: (i, 0)),
        ],
        out_shape=[
            jax.ShapeDtypeStruct((m, pool_width), _BF16),
            jax.ShapeDtypeStruct((m, conv_width), _BF16),
        ],
        scratch_shapes=[
            pltpu.VMEM((HALO + tm, d), _BF16),
            pltpu.VMEM((HALO + tm, pool_width), _F32),
            pltpu.VMEM((HALO + tm, pool_width), _F32),
            pltpu.VMEM((HALO + tm, conv_width), _F32),
            pltpu.VMEM((n_lane_blocks, HALO + tm, LANES), _F32),
            pltpu.VMEM((n_lane_blocks, tm, LANES), _F32),
        ],
        compiler_params=pltpu.CompilerParams(
            dimension_semantics=("parallel",), vmem_limit_bytes=VMEM_LIMIT_BYTES),
        name="mix_in",
    )(x, x, g, w_pc, b_pc, w_grp, pool_scale, conv_dw.reshape(n_taps, n_lane_blocks, LANES),
      conv_b.reshape(n_lane_blocks, LANES), ln_g, ln_b)


def _mix_out_kernel(x_ref, g_ref, wgate_ref, bgate_ref, mixed_ref, c_ref, pp_ref, cp_ref, wo_ref,
                    o_ref, m_ref, *, col_chunk):
    d = x_ref.shape[1]
    h = _rmsnorm(x_ref[...], g_ref[...]).astype(_BF16)
    mixed = mixed_ref[...]
    c = c_ref[...]
    for n in range(d // col_chunk):
        ca = slice(n * col_chunk, (n + 1) * col_chunk)
        cb = slice(d + n * col_chunk, d + (n + 1) * col_chunk)
        gate_a = jax.nn.sigmoid(_dot(h, wgate_ref[:, ca]) + bgate_ref[:, ca])
        gate_b = jax.nn.sigmoid(_dot(h, wgate_ref[:, cb]) + bgate_ref[:, cb])
        a = _dot(mixed, pp_ref[:, ca])
        bb = _dot(c, cp_ref[:, ca])
        m_ref[:, ca] = (gate_a * a + gate_b * bb).astype(_BF16)
    o_ref[...] = x_ref[...] + _dot(m_ref[...], wo_ref[...])


def _mix_out(x, g, w_gate, b_gate, mixed, c, pool_w_proj, conv_w_proj, w_out, *, tm=256):
    m, d = x.shape
    width = mixed.shape[1]

    def resident(shape):
        return pl.BlockSpec(shape, lambda i: (0,) * len(shape), pipeline_mode=pl.Buffered(1))

    return pl.pallas_call(
        functools.partial(_mix_out_kernel, col_chunk=512),
        grid=(m // tm,),
        in_specs=[
            pl.BlockSpec((tm, d), lambda i: (i, 0)),
            resident((1, d)),
            resident(w_gate.shape),
            resident(b_gate.shape),
            pl.BlockSpec((tm, width), lambda i: (i, 0)),
            pl.BlockSpec((tm, width), lambda i: (i, 0)),
            resident(pool_w_proj.shape),
            resident(conv_w_proj.shape),
            resident(w_out.shape),
        ],
        out_specs=pl.BlockSpec((tm, d), lambda i: (i, 0)),
        out_shape=jax.ShapeDtypeStruct((m, d), _F32),
        scratch_shapes=[pltpu.VMEM((tm, d), _BF16)],
        compiler_params=pltpu.CompilerParams(
            dimension_semantics=("parallel",), vmem_limit_bytes=VMEM_LIMIT_BYTES),
        name="mix_out",
    )(x, g, w_gate, b_gate, mixed, c, pool_w_proj, conv_w_proj, w_out)


def kernel(x, ffn1_norm, ffn1_w_in, ffn1_w_out, mix_norm, w_in, b_in, pool_w_grp, pool_scale,
           pool_w_proj, conv_dw, conv_b, conv_ln_g, conv_ln_b, conv_w_proj, w_out, ffn2_norm,
           ffn2_w_in, ffn2_w_out, final_norm):
    batch, seq, d = x.shape
    depth = ffn1_norm.shape[0]
    pool_width = pool_w_proj.shape[1]
    conv_width = conv_w_proj.shape[1]
    n_pc = pool_width + 2 * conv_width

    def row(v):
        return v.reshape(1, -1).astype(_F32)

    xf = x.reshape(batch * seq, d)
    for l in range(depth):
        xf = _ffn(xf, row(ffn1_norm[l]), ffn1_w_in[l].astype(_BF16), ffn1_w_out[l].astype(_BF16))
        mixed, c = _mix_in(
            xf, row(mix_norm[l]), w_in[l][:, :n_pc].astype(_BF16), row(b_in[l][:n_pc]),
            pool_w_grp[l].astype(_BF16), row(pool_scale[l]), conv_dw[l], row(conv_b[l]),
            row(conv_ln_g[l]), row(conv_ln_b[l]), seq=seq)
        xf = _mix_out(
            xf, row(mix_norm[l]), w_in[l][:, n_pc:].astype(_BF16), row(b_in[l][n_pc:]), mixed, c,
            pool_w_proj[l].astype(_BF16), conv_w_proj[l].astype(_BF16), w_out[l].astype(_BF16))
        last = l == depth - 1
        xf = _ffn(xf, row(ffn2_norm[l]), ffn2_w_in[l].astype(_BF16), ffn2_w_out[l].astype(_BF16),
                  row(final_norm) if last else None)
    if depth == 0:
        raise ValueError("depth must be positive")
    return xf.reshape(batch, seq, d)
```

```python
import functools
import math

import jax
import jax.numpy as jnp
from jax import lax
from jax.experimental import pallas as pl
from jax.experimental.pallas import tpu as pltpu

EPS = 1e-6
FFN_RESIDUAL_WEIGHT = 0.5
POOL_WINDOWS = (2, 4, 8, 16)

HALO = 32
SUBLANES = 8
LANES = 128
ROW_STRIDE = 4
CONV_ACCUMULATORS = 8
VMEM_LIMIT_BYTES = 56 * 1024 * 1024

_F32 = jnp.float32
_BF16 = jnp.bfloat16


def _rmsnorm(x, g):
    ms = jnp.mean(x * x, axis=-1, keepdims=True)
    return (x * lax.rsqrt(ms + EPS)) * g


def _dot(a, b):
    return jnp.dot(a, b, preferred_element_type=_F32)


def _resident(shape):
    return pl.BlockSpec(shape, lambda *_: (0,) * len(shape), pipeline_mode=pl.Buffered(1))


def _ffn_kernel(x_ref, g_ref, wg_ref, wu_ref, wo_ref, *rest, row_chunk, col_chunk, final_norm):
    if final_norm:
        fg_ref, o_ref, h_ref = rest
    else:
        o_ref, h_ref = rest
    j = pl.program_id(1)
    tm, d = x_ref.shape
    n_row_chunks = tm // row_chunk

    @pl.when(j == 0)
    def _():
        def body(r, carry):
            r0 = pl.multiple_of(r * row_chunk, row_chunk)
            x = x_ref[pl.ds(r0, row_chunk), :]
            h_ref[pl.ds(r0, row_chunk), :] = _rmsnorm(x, g_ref[...]).astype(_BF16)
            o_ref[pl.ds(r0, row_chunk), :] = jnp.zeros((row_chunk, d), _F32)
            return carry

        lax.fori_loop(0, n_row_chunks, body, 0)

    h = h_ref[...]
    gate = _dot(h, wg_ref[...])
    up = _dot(h, wu_ref[...])
    act = ((gate * jax.nn.sigmoid(gate)) * up).astype(_BF16)
    for n in range(d // col_chunk):
        cs = slice(n * col_chunk, (n + 1) * col_chunk)
        o_ref[:, cs] += _dot(act, wo_ref[:, cs])

    @pl.when(j == pl.num_programs(1) - 1)
    def _():
        def body(r, carry):
            r0 = pl.multiple_of(r * row_chunk, row_chunk)
            rows = pl.ds(r0, row_chunk)
            y = x_ref[rows, :] + FFN_RESIDUAL_WEIGHT * o_ref[rows, :]
            if final_norm:
                y = _rmsnorm(y, fg_ref[...])
            o_ref[rows, :] = y
            return carry

        lax.fori_loop(0, n_row_chunks, body, 0)


def _ffn(x, g, w_in, w_out, final_g=None, *, tm=1024, tf=512):
    m, d = x.shape
    f = w_out.shape[0]
    n_f = f // tf
    final_norm = final_g is not None
    in_specs = [
        pl.BlockSpec((tm, d), lambda i, j: (i, 0)),
        pl.BlockSpec((1, d), lambda i, j: (0, 0)),
        pl.BlockSpec((d, tf), lambda i, j: (0, j)),
        pl.BlockSpec((d, tf), lambda i, j: (0, j + n_f)),
        pl.BlockSpec((tf, d), lambda i, j: (j, 0)),
    ]
    args = [x, g, w_in, w_in, w_out]
    if final_norm:
        in_specs.append(pl.BlockSpec((1, d), lambda i, j: (0, 0)))
        args.append(final_g)
    return pl.pallas_call(
        functools.partial(_ffn_kernel, row_chunk=64, col_chunk=512, final_norm=final_norm),
        grid=(m // tm, n_f),
        in_specs=in_specs,
        out_specs=pl.BlockSpec((tm, d), lambda i, j: (i, 0)),
        out_shape=jax.ShapeDtypeStruct((m, d), _F32),
        scratch_shapes=[pltpu.VMEM((tm, d), _BF16)],
        compiler_params=pltpu.CompilerParams(
            dimension_semantics=("parallel", "arbitrary"), vmem_limit_bytes=VMEM_LIMIT_BYTES),
        name="ffn_final" if final_norm else "ffn",
    )(*args)


def _mix_in_kernel(x_ref, halo_ref, g_ref, w_ref, b_ref, wgrp_ref, pscale_ref, mixed_ref, glu_ref,
                   h_ref, u_ref, s_ref, t_ref, *, tiles_per_seq):
    i = pl.program_id(0)
    tm = x_ref.shape[0]
    pool_width = mixed_ref.shape[1]
    conv_width = glu_ref.shape[1]
    group = pool_width // len(POOL_WINDOWS)
    tile_in_seq = i % tiles_per_seq

    h_ref[0:HALO, :] = _rmsnorm(halo_ref[...], g_ref[...]).astype(_BF16)
    h_ref[HALO:HALO + tm, :] = _rmsnorm(x_ref[...], g_ref[...]).astype(_BF16)

    u = _dot(h_ref[...], w_ref[:, 0:pool_width]) + b_ref[:, 0:pool_width]
    u_ref[0:HALO, :] = jnp.where(tile_in_seq == 0, 0.0, u[0:HALO])
    u_ref[HALO:, :] = u[HALO:]

    pos = tile_in_seq * tm + lax.broadcasted_iota(jnp.int32, (tm, 1), 0)
    for gi, window in enumerate(POOL_WINDOWS):
        cs = slice(gi * group, (gi + 1) * group)
        src, shift, lo, level = u_ref, 1, 8, 0
        while shift < window:
            dst = (s_ref, t_ref)[level % 2]
            dst[lo:, cs] = src[lo:, cs] + src[lo - shift:HALO + tm - shift, cs]
            src, shift, lo, level = dst, shift * 2, lo + 8, level + 1
        inv_count = 1.0 / jnp.minimum(pos + 1, window).astype(_F32)
        pooled = src[HALO:, cs] * inv_count - u_ref[HALO:, cs]
        mixed = _dot(pooled.astype(_BF16), wgrp_ref[gi])
        mixed_ref[:, cs] = (mixed * pscale_ref[:, cs]).astype(_BF16)

    h = h_ref[HALO:, :]
    za = _dot(h, w_ref[:, pool_width:pool_width + conv_width]) \
        + b_ref[:, pool_width:pool_width + conv_width]
    zg = _dot(h, w_ref[:, pool_width + conv_width:]) + b_ref[:, pool_width + conv_width:]
    glu_ref[...] = za * jax.nn.sigmoid(zg)


def _mix_in(x, g, w_in, b_pc, w_grp, pool_scale, *, seq, tm=512):
    m, d = x.shape
    n_groups, group, _ = w_grp.shape
    pool_width = n_groups * group
    n_pc = b_pc.shape[1]
    conv_width = (n_pc - pool_width) // 2
    assert max(POOL_WINDOWS) <= HALO and seq % tm == 0 and tm % HALO == 0
    halo_blocks = tm // HALO

    return pl.pallas_call(
        functools.partial(_mix_in_kernel, tiles_per_seq=seq // tm),
        grid=(m // tm,),
        in_specs=[
            pl.BlockSpec((tm, d), lambda i: (i, 0)),
            pl.BlockSpec((HALO, d), lambda i: (jnp.maximum(i * halo_blocks - 1, 0), 0)),
            _resident((1, d)),
            _resident((d, n_pc)),
            _resident(b_pc.shape),
            _resident(w_grp.shape),
            _resident((1, pool_width)),
        ],
        out_specs=[
            pl.BlockSpec((tm, pool_width), lambda i: (i, 0)),
            pl.BlockSpec((tm, conv_width), lambda i: (i, 0)),
        ],
        out_shape=[
            jax.ShapeDtypeStruct((m, pool_width), _BF16),
            jax.ShapeDtypeStruct((m, conv_width), _F32),
        ],
        scratch_shapes=[
            pltpu.VMEM((HALO + tm, d), _BF16),
            pltpu.VMEM((HALO + tm, pool_width), _F32),
            pltpu.VMEM((HALO + tm, pool_width), _F32),
            pltpu.VMEM((HALO + tm, pool_width), _F32),
        ],
        compiler_params=pltpu.CompilerParams(
            dimension_semantics=("parallel",), vmem_limit_bytes=VMEM_LIMIT_BYTES),
        name="mix_in",
    )(x, x, g, w_in, b_pc, w_grp, pool_scale)


def _mix_out_kernel(x_ref, g_ref, wgate_refs, bgate_ref, mixed_ref, glu_ref, ghalo_ref, dw_ref,
                    cb_ref, lng_ref, lnb_ref, pp_ref, cp_ref, wo_ref, o_ref,
                    m_ref, gl_ref, conv_ref, c_ref, ga_ref, gb_ref, h_ref,
                    *, tiles_per_seq, col_chunk, ln_rows):
    i = pl.program_id(0)
    tm, d = x_ref.shape
    conv_width = glu_ref.shape[1]
    n_lane_blocks = conv_width // LANES
    n_taps = dw_ref.shape[0]
    first_in_seq = (i % tiles_per_seq) == 0

    for cb in range(n_lane_blocks):
        cs = slice(cb * LANES, (cb + 1) * LANES)
        gl_ref[cb, 0:HALO, :] = jnp.where(first_in_seq, 0.0, ghalo_ref[:, cs])
        gl_ref[cb, HALO:, :] = glu_ref[:, cs]

    rows_per_group = min(tm, CONV_ACCUMULATORS * SUBLANES)

    def conv_lane_block(cb):
        bias = cb_ref[pl.ds(cb, SUBLANES, stride=0), :]
        bits = None
        for r0 in range(0, tm, rows_per_group):
            starts = [r0 + blk * SUBLANES * ROW_STRIDE + ph
                      for blk in range(rows_per_group // (SUBLANES * ROW_STRIDE))
                      for ph in range(ROW_STRIDE)]
            accs = [bias for _ in starts]
            for t in range(n_taps):
                w = dw_ref[t, pl.ds(cb, SUBLANES, stride=0), :]
                off = HALO - (n_taps - 1) + t
                accs = [acc + w * gl_ref[cb, pl.ds(s + off, SUBLANES, stride=ROW_STRIDE), :]
                        for acc, s in zip(accs, starts)]
            for acc, s in zip(accs, starts):
                conv_ref[cb, pl.ds(s, SUBLANES, stride=ROW_STRIDE), :] = acc
                word = lax.bitcast_convert_type(acc, jnp.uint32)
                bits = word if bits is None else bits | word
        zero = lax.shift_right_logical(lax.shift_right_logical(bits, jnp.uint32(16)), jnp.uint32(16))
        return lax.bitcast_convert_type(zero, _F32)[0:1, :]

    h_ref[...] = _rmsnorm(x_ref[...], g_ref[...]).astype(_BF16)
    gate_block = wgate_refs[0].shape[1]

    def gate_weight(col):
        ref, lo = wgate_refs[col // gate_block], col % gate_block
        return ref[:, lo:lo + col_chunk]

    n_chunks = d // col_chunk
    convs_per_chunk = n_lane_blocks // n_chunks
    for n in range(n_chunks):
        ca = slice(n * col_chunk, (n + 1) * col_chunk)
        cb = slice(d + n * col_chunk, d + (n + 1) * col_chunk)
        zeros = [jnp.tile(conv_lane_block(n * convs_per_chunk + k), (1, col_chunk // LANES))
                 for k in range(convs_per_chunk)]
        h = h_ref[...]
        gate_a = jax.nn.sigmoid(_dot(h, gate_weight(ca.start)) + (bgate_ref[:, ca] + zeros[0]))
        ga_ref[:, ca] = gate_a * _dot(mixed_ref[...], pp_ref[:, ca])
        gate_b = jax.nn.sigmoid(_dot(h, gate_weight(cb.start)) + (bgate_ref[:, cb] + zeros[-1]))
        gb_ref[:, ca] = gate_b

    for r0 in range(0, tm, ln_rows):
        y = [conv_ref[cb, r0:r0 + ln_rows, :] for cb in range(n_lane_blocks)]
        mu = jnp.sum(sum(y), axis=-1, keepdims=True) * (1.0 / conv_width)
        yc = [v - mu for v in y]
        var = jnp.sum(sum(v * v for v in yc), axis=-1, keepdims=True) * (1.0 / conv_width)
        inv = lax.rsqrt(var + EPS)
        for cb in range(n_lane_blocks):
            cs = slice(cb * LANES, (cb + 1) * LANES)
            z = (yc[cb] * inv) * lng_ref[:, cs] + lnb_ref[:, cs]
            c_ref[r0:r0 + ln_rows, cs] = (z * jax.nn.sigmoid(z)).astype(_BF16)

    c = c_ref[...]
    for n in range(n_chunks):
        ca = slice(n * col_chunk, (n + 1) * col_chunk)
        bb = _dot(c, cp_ref[:, ca])
        m_ref[:, ca] = (ga_ref[:, ca] + gb_ref[:, ca] * bb).astype(_BF16)
    o_ref[...] = x_ref[...] + _dot(m_ref[...], wo_ref[...])


def _mix_out(x, g, w_in, gate_col, b_gate, mixed, glu, conv_dw, conv_b, ln_g, ln_b, pool_w_proj,
             conv_w_proj, w_out, *, seq, tm=256):
    m, d = x.shape
    pool_width = mixed.shape[1]
    n_taps, conv_width = conv_dw.shape
    assert n_taps - 1 <= HALO and seq % tm == 0 and tm % (SUBLANES * ROW_STRIDE) == 0
    halo_blocks = tm // HALO
    n_lane_blocks = conv_width // LANES
    col_chunk = 512
    gate_block = math.gcd(gate_col, d)
    n_gate_blocks = 2 * d // gate_block
    assert gate_block % col_chunk == 0 and gate_col + 2 * d == w_in.shape[1]
    assert d % col_chunk == 0 and n_lane_blocks == 2 * (d // col_chunk)

    return pl.pallas_call(
        functools.partial(_mix_out_kernel, tiles_per_seq=seq // tm, col_chunk=col_chunk, ln_rows=32),
        grid=(m // tm,),
        in_specs=[
            pl.BlockSpec((tm, d), lambda i: (i, 0)),
            _resident((1, d)),
            [pl.BlockSpec((d, gate_block), lambda i, k=k: (0, gate_col // gate_block + k),
                          pipeline_mode=pl.Buffered(1)) for k in range(n_gate_blocks)],
            _resident(b_gate.shape),
            pl.BlockSpec((tm, pool_width), lambda i: (i, 0)),
            pl.BlockSpec((tm, conv_width), lambda i: (i, 0)),
            pl.BlockSpec((HALO, conv_width), lambda i: (jnp.maximum(i * halo_blocks - 1, 0), 0)),
            _resident((n_taps, n_lane_blocks, LANES)),
            _resident((n_lane_blocks, LANES)),
            _resident((1, conv_width)),
            _resident((1, conv_width)),
            _resident(pool_w_proj.shape),
            _resident(conv_w_proj.shape),
            _resident(w_out.shape),
        ],
        out_specs=pl.BlockSpec((tm, d), lambda i: (i, 0)),
        out_shape=jax.ShapeDtypeStruct((m, d), _F32),
        scratch_shapes=[
            pltpu.VMEM((tm, d), _BF16),
            pltpu.VMEM((n_lane_blocks, HALO + tm, LANES), _F32),
            pltpu.VMEM((n_lane_blocks, tm, LANES), _F32),
            pltpu.VMEM((tm, conv_width), _BF16),
            pltpu.VMEM((tm, d), _F32),
            pltpu.VMEM((tm, d), _F32),
            pltpu.VMEM((tm, d), _BF16),
        ],
        compiler_params=pltpu.CompilerParams(
            dimension_semantics=("parallel",), vmem_limit_bytes=VMEM_LIMIT_BYTES),
        name="mix_out",
    )(x, g, [w_in] * n_gate_blocks, b_gate, mixed, glu, glu, conv_dw.reshape(n_taps, n_lane_blocks, LANES),
      conv_b.reshape(n_lane_blocks, LANES), ln_g, ln_b, pool_w_proj, conv_w_proj, w_out)


def kernel(x, ffn1_norm, ffn1_w_in, ffn1_w_out, mix_norm, w_in, b_in, pool_w_grp, pool_scale,
           pool_w_proj, conv_dw, conv_b, conv_ln_g, conv_ln_b, conv_w_proj, w_out, ffn2_norm,
           ffn2_w_in, ffn2_w_out, final_norm):
    batch, seq, d = x.shape
    depth = ffn1_norm.shape[0]
    if depth == 0:
        raise ValueError("depth must be positive")
    pool_width = pool_w_proj.shape[1]
    conv_width = conv_w_proj.shape[1]
    n_pc = pool_width + 2 * conv_width

    def row(v):
        return v.reshape(1, -1).astype(_F32)

    xf = x.reshape(batch * seq, d)
    for l in range(depth):
        xf = _ffn(xf, row(ffn1_norm[l]), ffn1_w_in[l].astype(_BF16), ffn1_w_out[l].astype(_BF16))
        w_in_bf16 = w_in[l].astype(_BF16)
        mixed, glu = _mix_in(
            xf, row(mix_norm[l]), w_in_bf16, row(b_in[l][:n_pc]),
            pool_w_grp[l].astype(_BF16), row(pool_scale[l]), seq=seq)
        xf = _mix_out(
            xf, row(mix_norm[l]), w_in_bf16, n_pc, row(b_in[l][n_pc:]), mixed, glu,
            conv_dw[l], conv_b[l], row(conv_ln_g[l]), row(conv_ln_b[l]),
            pool_w_proj[l].astype(_BF16), conv_w_proj[l].astype(_BF16), w_out[l].astype(_BF16),
            seq=seq)
        final_g = row(final_norm) if l == depth - 1 else None
        xf = _ffn(xf, row(ffn2_norm[l]), ffn2_w_in[l].astype(_BF16), ffn2_w_out[l].astype(_BF16),
                  final_g)
    return xf.reshape(batch, seq, d)
```

```python
import functools
import math

import jax
import jax.numpy as jnp
from jax import lax
from jax.experimental import pallas as pl
from jax.experimental.pallas import tpu as pltpu

EPS = 1e-6
FFN_RESIDUAL_WEIGHT = 0.5
POOL_WINDOWS = (2, 4, 8, 16)

HALO = 32
SUBLANES = 8
LANES = 128
ROW_STRIDE = 4
CONV_ACCUMULATORS = 8
VMEM_LIMIT_BYTES = 56 * 1024 * 1024

_F32 = jnp.float32
_BF16 = jnp.bfloat16


def _rmsnorm(x, g):
    ms = jnp.mean(x * x, axis=-1, keepdims=True)
    return (x * lax.rsqrt(ms + EPS)) * g


def _dot(a, b):
    return jnp.dot(a, b, preferred_element_type=_F32)


def _resident(shape):
    return pl.BlockSpec(shape, lambda *_: (0,) * len(shape), pipeline_mode=pl.Buffered(1))


def _ffn_kernel(x_ref, g_ref, wg_ref, wu_ref, wo_ref, *rest, row_chunk, col_chunk, final_norm):
    if final_norm:
        fg_ref, o_ref, h_ref = rest
    else:
        o_ref, h_ref = rest
    j = pl.program_id(1)
    tm, d = x_ref.shape
    n_row_chunks = tm // row_chunk

    @pl.when(j == 0)
    def _():
        def body(r, carry):
            r0 = pl.multiple_of(r * row_chunk, row_chunk)
            x = x_ref[pl.ds(r0, row_chunk), :]
            h_ref[pl.ds(r0, row_chunk), :] = _rmsnorm(x, g_ref[...]).astype(_BF16)
            o_ref[pl.ds(r0, row_chunk), :] = jnp.zeros((row_chunk, d), _F32)
            return carry

        lax.fori_loop(0, n_row_chunks, body, 0)

    h = h_ref[...]
    gate = _dot(h, wg_ref[...])
    up = _dot(h, wu_ref[...])
    act = ((gate * jax.nn.sigmoid(gate)) * up).astype(_BF16)
    for n in range(d // col_chunk):
        cs = slice(n * col_chunk, (n + 1) * col_chunk)
        o_ref[:, cs] += _dot(act, wo_ref[:, cs])

    @pl.when(j == pl.num_programs(1) - 1)
    def _():
        def body(r, carry):
            r0 = pl.multiple_of(r * row_chunk, row_chunk)
            rows = pl.ds(r0, row_chunk)
            y = x_ref[rows, :] + FFN_RESIDUAL_WEIGHT * o_ref[rows, :]
            if final_norm:
                y = _rmsnorm(y, fg_ref[...])
            o_ref[rows, :] = y
            return carry

        lax.fori_loop(0, n_row_chunks, body, 0)


def _ffn(x, g, w_in, w_out, final_g=None, *, tm=1024, tf=512):
    m, d = x.shape
    f = w_out.shape[0]
    n_f = f // tf
    final_norm = final_g is not None
    in_specs = [
        pl.BlockSpec((tm, d), lambda i, j: (i, 0)),
        pl.BlockSpec((1, d), lambda i, j: (0, 0)),
        pl.BlockSpec((d, tf), lambda i, j: (0, j)),
        pl.BlockSpec((d, tf), lambda i, j: (0, j + n_f)),
        pl.BlockSpec((tf, d), lambda i, j: (j, 0)),
    ]
    args = [x, g, w_in, w_in, w_out]
    if final_norm:
        in_specs.append(pl.BlockSpec((1, d), lambda i, j: (0, 0)))
        args.append(final_g)
    return pl.pallas_call(
        functools.partial(_ffn_kernel, row_chunk=128, col_chunk=512, final_norm=final_norm),
        grid=(m // tm, n_f),
        in_specs=in_specs,
        out_specs=pl.BlockSpec((tm, d), lambda i, j: (i, 0)),
        out_shape=jax.ShapeDtypeStruct((m, d), _F32),
        scratch_shapes=[pltpu.VMEM((tm, d), _BF16)],
        compiler_params=pltpu.CompilerParams(
            dimension_semantics=("parallel", "arbitrary"), vmem_limit_bytes=VMEM_LIMIT_BYTES),
        name="ffn_final" if final_norm else "ffn",
    )(*args)


def _mix_in_kernel(x_ref, halo_ref, g_ref, w_ref, b_ref, wgrp_ref, pscale_ref, mixed_ref, glu_ref,
                   h_ref, u_ref, s_ref, t_ref, *, tiles_per_seq):
    i = pl.program_id(0)
    tm = x_ref.shape[0]
    pool_width = mixed_ref.shape[1]
    conv_width = glu_ref.shape[1]
    group = pool_width // len(POOL_WINDOWS)
    tile_in_seq = i % tiles_per_seq

    h_ref[0:HALO, :] = _rmsnorm(halo_ref[...], g_ref[...]).astype(_BF16)
    h_ref[HALO:HALO + tm, :] = _rmsnorm(x_ref[...], g_ref[...]).astype(_BF16)

    u = _dot(h_ref[...], w_ref[:, 0:pool_width]) + b_ref[:, 0:pool_width]
    u_ref[0:HALO, :] = jnp.where(tile_in_seq == 0, 0.0, u[0:HALO])
    u_ref[HALO:, :] = u[HALO:]

    pos = tile_in_seq * tm + lax.broadcasted_iota(jnp.int32, (tm, 1), 0)
    for gi, window in enumerate(POOL_WINDOWS):
        cs = slice(gi * group, (gi + 1) * group)
        src, shift, lo, level = u_ref, 1, 8, 0
        while shift < window:
            dst = (s_ref, t_ref)[level % 2]
            dst[lo:, cs] = src[lo:, cs] + src[lo - shift:HALO + tm - shift, cs]
            src, shift, lo, level = dst, shift * 2, lo + 8, level + 1
        inv_count = 1.0 / jnp.minimum(pos + 1, window).astype(_F32)
        pooled = src[HALO:, cs] * inv_count - u_ref[HALO:, cs]
        mixed = _dot(pooled.astype(_BF16), wgrp_ref[gi])
        mixed_ref[:, cs] = (mixed * pscale_ref[:, cs]).astype(_BF16)

    h = h_ref[HALO:, :]
    za = _dot(h, w_ref[:, pool_width:pool_width + conv_width]) \
        + b_ref[:, pool_width:pool_width + conv_width]
    zg = _dot(h, w_ref[:, pool_width + conv_width:]) + b_ref[:, pool_width + conv_width:]
    glu_ref[...] = za * jax.nn.sigmoid(zg)


def _mix_in(x, g, w_in, b_pc, w_grp, pool_scale, *, seq, tm=512):
    m, d = x.shape
    n_groups, group, _ = w_grp.shape
    pool_width = n_groups * group
    n_pc = b_pc.shape[1]
    conv_width = (n_pc - pool_width) // 2
    assert max(POOL_WINDOWS) <= HALO and seq % tm == 0 and tm % HALO == 0
    halo_blocks = tm // HALO

    return pl.pallas_call(
        functools.partial(_mix_in_kernel, tiles_per_seq=seq // tm),
        grid=(m // tm,),
        in_specs=[
            pl.BlockSpec((tm, d), lambda i: (i, 0)),
            pl.BlockSpec((HALO, d), lambda i: (jnp.maximum(i * halo_blocks - 1, 0), 0)),
            _resident((1, d)),
            _resident((d, n_pc)),
            _resident(b_pc.shape),
            _resident(w_grp.shape),
            _resident((1, pool_width)),
        ],
        out_specs=[
            pl.BlockSpec((tm, pool_width), lambda i: (i, 0)),
            pl.BlockSpec((tm, conv_width), lambda i: (i, 0)),
        ],
        out_shape=[
            jax.ShapeDtypeStruct((m, pool_width), _BF16),
            jax.ShapeDtypeStruct((m, conv_width), _F32),
        ],
        scratch_shapes=[
            pltpu.VMEM((HALO + tm, d), _BF16),
            pltpu.VMEM((HALO + tm, pool_width), _F32),
            pltpu.VMEM((HALO + tm, pool_width), _F32),
            pltpu.VMEM((HALO + tm, pool_width), _F32),
        ],
        compiler_params=pltpu.CompilerParams(
            dimension_semantics=("parallel",), vmem_limit_bytes=VMEM_LIMIT_BYTES),
        name="mix_in",
    )(x, x, g, w_in, b_pc, w_grp, pool_scale)


def _mix_out_kernel(x_ref, g_ref, wgate_refs, bgate_ref, mixed_ref, glu_ref, ghalo_ref, dw_ref,
                    cb_ref, lng_ref, lnb_ref, pp_ref, cp_ref, wo_ref, o_ref,
                    m_ref, gl_ref, conv_ref, c_ref, ga_ref, gb_ref, h_ref,
                    *, tiles_per_seq, col_chunk, ln_rows):
    i = pl.program_id(0)
    tm, d = x_ref.shape
    conv_width = glu_ref.shape[1]
    n_lane_blocks = conv_width // LANES
    n_taps = dw_ref.shape[0]
    first_in_seq = (i % tiles_per_seq) == 0

    for cb in range(n_lane_blocks):
        cs = slice(cb * LANES, (cb + 1) * LANES)
        gl_ref[cb, 0:HALO, :] = jnp.where(first_in_seq, 0.0, ghalo_ref[:, cs])
        gl_ref[cb, HALO:, :] = glu_ref[:, cs]

    rows_per_group = min(tm, CONV_ACCUMULATORS * SUBLANES)

    def conv_lane_block(cb):
        bias = cb_ref[pl.ds(cb, SUBLANES, stride=0), :]
        for r0 in range(0, tm, rows_per_group):
            starts = [r0 + blk * SUBLANES * ROW_STRIDE + ph
                      for blk in range(rows_per_group // (SUBLANES * ROW_STRIDE))
                      for ph in range(ROW_STRIDE)]
            accs = [bias for _ in starts]
            for t in range(n_taps):
                w = dw_ref[t, pl.ds(cb, SUBLANES, stride=0), :]
                off = HALO - (n_taps - 1) + t
                accs = [acc + w * gl_ref[cb, pl.ds(s + off, SUBLANES, stride=ROW_STRIDE), :]
                        for acc, s in zip(accs, starts)]
            for acc, s in zip(accs, starts):
                conv_ref[cb, pl.ds(s, SUBLANES, stride=ROW_STRIDE), :] = acc

    h_ref[...] = _rmsnorm(x_ref[...], g_ref[...]).astype(_BF16)
    gate_block = wgate_refs[0].shape[1]

    def gate_weight(col):
        ref, lo = wgate_refs[col // gate_block], col % gate_block
        return ref[:, lo:lo + col_chunk]

    n_chunks = d // col_chunk
    convs_per_chunk = n_lane_blocks // n_chunks
    for n in range(n_chunks):
        ca = slice(n * col_chunk, (n + 1) * col_chunk)
        cb = slice(d + n * col_chunk, d + (n + 1) * col_chunk)
        for k in range(convs_per_chunk):
            conv_lane_block(n * convs_per_chunk + k)
        h = h_ref[...]
        gate_a = jax.nn.sigmoid(_dot(h, gate_weight(ca.start)) + bgate_ref[:, ca])
        ga_ref[:, ca] = gate_a * _dot(mixed_ref[...], pp_ref[:, ca])
        gb_ref[:, ca] = jax.nn.sigmoid(_dot(h, gate_weight(cb.start)) + bgate_ref[:, cb])

    for r0 in range(0, tm, ln_rows):
        y = [conv_ref[cb, r0:r0 + ln_rows, :] for cb in range(n_lane_blocks)]
        mu = jnp.sum(sum(y), axis=-1, keepdims=True) * (1.0 / conv_width)
        yc = [v - mu for v in y]
        var = jnp.sum(sum(v * v for v in yc), axis=-1, keepdims=True) * (1.0 / conv_width)
        inv = lax.rsqrt(var + EPS)
        for cb in range(n_lane_blocks):
            cs = slice(cb * LANES, (cb + 1) * LANES)
            z = (yc[cb] * inv) * lng_ref[:, cs] + lnb_ref[:, cs]
            c_ref[r0:r0 + ln_rows, cs] = (z * jax.nn.sigmoid(z)).astype(_BF16)

    c = c_ref[...]
    for n in range(n_chunks):
        ca = slice(n * col_chunk, (n + 1) * col_chunk)
        bb = _dot(c, cp_ref[:, ca])
        m_ref[:, ca] = (ga_ref[:, ca] + gb_ref[:, ca] * bb).astype(_BF16)
    o_ref[...] = x_ref[...] + _dot(m_ref[...], wo_ref[...])


def _mix_out(x, g, w_in, gate_col, b_gate, mixed, glu, conv_dw, conv_b, ln_g, ln_b, pool_w_proj,
             conv_w_proj, w_out, *, seq, tm=256):
    m, d = x.shape
    pool_width = mixed.shape[1]
    n_taps, conv_width = conv_dw.shape
    assert n_taps - 1 <= HALO and seq % tm == 0 and tm % (SUBLANES * ROW_STRIDE) == 0
    halo_blocks = tm // HALO
    n_lane_blocks = conv_width // LANES
    col_chunk = 512
    gate_block = math.gcd(gate_col, d)
    n_gate_blocks = 2 * d // gate_block
    assert gate_block % col_chunk == 0 and gate_col + 2 * d == w_in.shape[1]
    assert d % col_chunk == 0 and n_lane_blocks == 2 * (d // col_chunk)

    return pl.pallas_call(
        functools.partial(_mix_out_kernel, tiles_per_seq=seq // tm, col_chunk=col_chunk, ln_rows=32),
        grid=(m // tm,),
        in_specs=[
            pl.BlockSpec((tm, d), lambda i: (i, 0)),
            _resident((1, d)),
            [pl.BlockSpec((d, gate_block), lambda i, k=k: (0, gate_col // gate_block + k),
                          pipeline_mode=pl.Buffered(1)) for k in range(n_gate_blocks)],
            _resident(b_gate.shape),
            pl.BlockSpec((tm, pool_width), lambda i: (i, 0)),
            pl.BlockSpec((tm, conv_width), lambda i: (i, 0)),
            pl.BlockSpec((HALO, conv_width), lambda i: (jnp.maximum(i * halo_blocks - 1, 0), 0)),
            _resident((n_taps, n_lane_blocks, LANES)),
            _resident((n_lane_blocks, LANES)),
            _resident((1, conv_width)),
            _resident((1, conv_width)),
            _resident(pool_w_proj.shape),
            _resident(conv_w_proj.shape),
            _resident(w_out.shape),
        ],
        out_specs=pl.BlockSpec((tm, d), lambda i: (i, 0)),
        out_shape=jax.ShapeDtypeStruct((m, d), _F32),
        scratch_shapes=[
            pltpu.VMEM((tm, d), _BF16),
            pltpu.VMEM((n_lane_blocks, HALO + tm, LANES), _F32),
            pltpu.VMEM((n_lane_blocks, tm, LANES), _F32),
            pltpu.VMEM((tm, conv_width), _BF16),
            pltpu.VMEM((tm, d), _F32),
            pltpu.VMEM((tm, d), _F32),
            pltpu.VMEM((tm, d), _BF16),
        ],
        compiler_params=pltpu.CompilerParams(
            dimension_semantics=("parallel",), vmem_limit_bytes=VMEM_LIMIT_BYTES),
        name="mix_out",
    )(x, g, [w_in] * n_gate_blocks, b_gate, mixed, glu, glu, conv_dw.reshape(n_taps, n_lane_blocks, LANES),
      conv_b.reshape(n_lane_blocks, LANES), ln_g, ln_b, pool_w_proj, conv_w_proj, w_out)


def kernel(x, ffn1_norm, ffn1_w_in, ffn1_w_out, mix_norm, w_in, b_in, pool_w_grp, pool_scale,
           pool_w_proj, conv_dw, conv_b, conv_ln_g, conv_ln_b, conv_w_proj, w_out, ffn2_norm,
           ffn2_w_in, ffn2_w_out, final_norm):
    batch, seq, d = x.shape
    depth = ffn1_norm.shape[0]
    if depth == 0:
        raise ValueError("depth must be positive")
    pool_width = pool_w_proj.shape[1]
    conv_width = conv_w_proj.shape[1]
    n_pc = pool_width + 2 * conv_width

    def row(v):
        return v.reshape(1, -1).astype(_F32)

    xf = x.reshape(batch * seq, d)
    for l in range(depth):
        xf = _ffn(xf, row(ffn1_norm[l]), ffn1_w_in[l].astype(_BF16), ffn1_w_out[l].astype(_BF16))
        w_in_bf16 = w_in[l].astype(_BF16)
        mixed, glu = _mix_in(
            xf, row(mix_norm[l]), w_in_bf16, row(b_in[l][:n_pc]),
            pool_w_grp[l].astype(_BF16), row(pool_scale[l]), seq=seq)
        xf = _mix_out(
            xf, row(mix_norm[l]), w_in_bf16, n_pc, row(b_in[l][n_pc:]), mixed, glu,
            conv_dw[l], conv_b[l], row(conv_ln_g[l]), row(conv_ln_b[l]),
            pool_w_proj[l].astype(_BF16), conv_w_proj[l].astype(_BF16), w_out[l].astype(_BF16),
            seq=seq)
        final_g = row(final_norm) if l == depth - 1 else None
        xf = _ffn(xf, row(ffn2_norm[l]), ffn2_w_in[l].astype(_BF16), ffn2_w_out[l].astype(_BF16),
                  final_g)
    return xf.reshape(batch, seq, d)
```

```python
import functools
import math

import jax
import jax.numpy as jnp
from jax import lax
from jax.experimental import pallas as pl
from jax.experimental.pallas import tpu as pltpu

EPS = 1e-6
FFN_RESIDUAL_WEIGHT = 0.5
POOL_WINDOWS = (2, 4, 8, 16)

HALO = 32
SUBLANES = 8
LANES = 128
ROW_STRIDE = 4
CONV_ACCUMULATORS = 8
VMEM_LIMIT_BYTES = 58 * 1024 * 1024

_F32 = jnp.float32
_BF16 = jnp.bfloat16


def _rmsnorm(x, g):
    ms = jnp.mean(x * x, axis=-1, keepdims=True)
    return (x * lax.rsqrt(ms + EPS)) * g


def _dot(a, b):
    return jnp.dot(a, b, preferred_element_type=_F32)


def _resident(shape):
    return pl.BlockSpec(shape, lambda *_: (0,) * len(shape), pipeline_mode=pl.Buffered(1))


def _ffn_body(j, n_j, x_ref, g_ref, fg_ref, wg_ref, wu_ref, wo_ref, o_ref, h_ref,
              *, row_chunk, col_chunk):
    tm, d = x_ref.shape
    n_row_chunks = tm // row_chunk

    @pl.when(j == 0)
    def _():
        def body(r, carry):
            r0 = pl.multiple_of(r * row_chunk, row_chunk)
            x = x_ref[pl.ds(r0, row_chunk), :]
            h_ref[pl.ds(r0, row_chunk), :] = _rmsnorm(x, g_ref[...]).astype(_BF16)
            o_ref[pl.ds(r0, row_chunk), :] = jnp.zeros((row_chunk, d), _F32)
            return carry

        lax.fori_loop(0, n_row_chunks, body, 0)

    h = h_ref[...]
    gate = _dot(h, wg_ref[...])
    up = _dot(h, wu_ref[...])
    act = ((gate * jax.nn.sigmoid(gate)) * up).astype(_BF16)
    for n in range(d // col_chunk):
        cs = slice(n * col_chunk, (n + 1) * col_chunk)
        o_ref[:, cs] += _dot(act, wo_ref[:, cs])

    @pl.when(j == n_j - 1)
    def _():
        def body(r, carry):
            r0 = pl.multiple_of(r * row_chunk, row_chunk)
            rows = pl.ds(r0, row_chunk)
            y = x_ref[rows, :] + FFN_RESIDUAL_WEIGHT * o_ref[rows, :]
            if fg_ref is not None:
                y = _rmsnorm(y, fg_ref[...])
            o_ref[rows, :] = y
            return carry

        lax.fori_loop(0, n_row_chunks, body, 0)


def _ffn_head_kernel(x_ref, g_ref, wg32_ref, wu32_ref, wo32_ref, *rest, final_norm, **chunks):
    fg_ref = rest[0] if final_norm else None
    o_ref, wg_ref, wu_ref, wo_ref, h_ref = rest[1:] if final_norm else rest
    wg_ref[...] = wg32_ref[...].astype(_BF16)
    wu_ref[...] = wu32_ref[...].astype(_BF16)
    wo_ref[...] = wo32_ref[...].astype(_BF16)
    _ffn_body(pl.program_id(0), pl.num_programs(0), x_ref, g_ref, fg_ref, wg_ref, wu_ref, wo_ref,
              o_ref, h_ref, **chunks)


def _ffn_tail_kernel(x_ref, y0_ref, g_ref, wg_ref, wu_ref, wo_ref, *rest, final_norm, **chunks):
    fg_ref = rest[0] if final_norm else None
    o_ref, h_ref = rest[1:] if final_norm else rest
    i, j = pl.program_id(0), pl.program_id(1)
    copy_cols = y0_ref.shape[1]

    @pl.when(i == 0)
    def _():
        for k in range(o_ref.shape[1] // copy_cols):
            @pl.when(j == k)
            def _(k=k):
                o_ref[:, k * copy_cols:(k + 1) * copy_cols] = y0_ref[...]

    @pl.when(i > 0)
    def _():
        _ffn_body(j, pl.num_programs(1), x_ref, g_ref, fg_ref, wg_ref, wu_ref, wo_ref, o_ref, h_ref,
                  **chunks)


def _ffn(x, g, w_in, w_out, final_g=None, *, tm=1024, tf=512, tf_head=256):
    m, d = x.shape
    f = w_out.shape[0]
    final_norm = final_g is not None
    chunks = dict(row_chunk=128, col_chunk=512)
    extra_specs = [pl.BlockSpec((1, d), lambda *_: (0, 0))] if final_norm else []
    extra_args = [final_g] if final_norm else []
    suffix = "_final" if final_norm else ""

    n_head = f // tf_head
    y, wg, wu, wo = pl.pallas_call(
        functools.partial(_ffn_head_kernel, final_norm=final_norm, **chunks),
        grid=(n_head,),
        in_specs=[
            pl.BlockSpec((tm, d), lambda j: (0, 0), pipeline_mode=pl.Buffered(1)),
            pl.BlockSpec((1, d), lambda j: (0, 0)),
            pl.BlockSpec((d, tf_head), lambda j: (0, j)),
            pl.BlockSpec((d, tf_head), lambda j: (0, j + n_head)),
            pl.BlockSpec((tf_head, d), lambda j: (j, 0)),
        ] + extra_specs,
        out_specs=[
            pl.BlockSpec((tm, d), lambda j: (0, 0)),
            pl.BlockSpec((d, tf_head), lambda j: (0, j)),
            pl.BlockSpec((d, tf_head), lambda j: (0, j)),
            pl.BlockSpec((tf_head, d), lambda j: (j, 0)),
        ],
        out_shape=[
            jax.ShapeDtypeStruct((tm, d), _F32),
            jax.ShapeDtypeStruct((d, f), _BF16),
            jax.ShapeDtypeStruct((d, f), _BF16),
            jax.ShapeDtypeStruct((f, d), _BF16),
        ],
        scratch_shapes=[pltpu.VMEM((tm, d), _BF16)],
        compiler_params=pltpu.CompilerParams(
            dimension_semantics=("arbitrary",), vmem_limit_bytes=VMEM_LIMIT_BYTES),
        name="ffn_head" + suffix,
    )(x, g, w_in, w_in, w_out, *extra_args)

    n_f = f // tf
    copy_cols = 256
    n_copy = d // copy_cols
    assert n_copy <= n_f

    def chunk(i, j):
        return jnp.where(i == 0, 0, j)

    return pl.pallas_call(
        functools.partial(_ffn_tail_kernel, final_norm=final_norm, **chunks),
        grid=(m // tm, n_f),
        in_specs=[
            pl.BlockSpec((tm, d), lambda i, j: (i, 0)),
            pl.BlockSpec((tm, copy_cols),
                         lambda i, j: (0, jnp.where(i == 0, jnp.minimum(j, n_copy - 1), n_copy - 1))),
            pl.BlockSpec((1, d), lambda i, j: (0, 0)),
            pl.BlockSpec((d, tf), lambda i, j: (0, chunk(i, j))),
            pl.BlockSpec((d, tf), lambda i, j: (0, chunk(i, j))),
            pl.BlockSpec((tf, d), lambda i, j: (chunk(i, j), 0)),
        ] + extra_specs,
        out_specs=pl.BlockSpec((tm, d), lambda i, j: (i, 0)),
        out_shape=jax.ShapeDtypeStruct((m, d), _F32),
        scratch_shapes=[pltpu.VMEM((tm, d), _BF16)],
        compiler_params=pltpu.CompilerParams(
            dimension_semantics=("parallel", "arbitrary"), vmem_limit_bytes=VMEM_LIMIT_BYTES),
        name="ffn" + suffix,
    )(x, y, g, wg, wu, wo, *extra_args)


def _mix_in_kernel(x_ref, halo_ref, g_ref, w_ref, b_ref, wgrp_ref, pscale_ref, mixed_ref, glu_ref,
                   h_ref, u_ref, s_ref, t_ref, *, tiles_per_seq):
    i = pl.program_id(0)
    tm = x_ref.shape[0]
    pool_width = mixed_ref.shape[1]
    conv_width = glu_ref.shape[1]
    group = pool_width // len(POOL_WINDOWS)
    tile_in_seq = i % tiles_per_seq

    h_ref[0:HALO, :] = _rmsnorm(halo_ref[...], g_ref[...]).astype(_BF16)
    h_ref[HALO:HALO + tm, :] = _rmsnorm(x_ref[...], g_ref[...]).astype(_BF16)

    u = _dot(h_ref[...], w_ref[:, 0:pool_width]) + b_ref[:, 0:pool_width]
    u_ref[0:HALO, :] = jnp.where(tile_in_seq == 0, 0.0, u[0:HALO])
    u_ref[HALO:, :] = u[HALO:]

    pos = tile_in_seq * tm + lax.broadcasted_iota(jnp.int32, (tm, 1), 0)
    for gi, window in enumerate(POOL_WINDOWS):
        cs = slice(gi * group, (gi + 1) * group)
        src, shift, lo, level = u_ref, 1, 8, 0
        while shift < window:
            dst = (s_ref, t_ref)[level % 2]
            dst[lo:, cs] = src[lo:, cs] + src[lo - shift:HALO + tm - shift, cs]
            src, shift, lo, level = dst, shift * 2, lo + 8, level + 1
        inv_count = 1.0 / jnp.minimum(pos + 1, window).astype(_F32)
        pooled = src[HALO:, cs] * inv_count - u_ref[HALO:, cs]
        mixed = _dot(pooled.astype(_BF16), wgrp_ref[gi])
        mixed_ref[:, cs] = (mixed * pscale_ref[:, cs]).astype(_BF16)

    h = h_ref[HALO:, :]
    za = _dot(h, w_ref[:, pool_width:pool_width + conv_width]) \
        + b_ref[:, pool_width:pool_width + conv_width]
    zg = _dot(h, w_ref[:, pool_width + conv_width:]) + b_ref[:, pool_width + conv_width:]
    glu_ref[...] = za * jax.nn.sigmoid(zg)


def _mix_in(x, g, w_in, b_pc, w_grp, pool_scale, *, seq, tm=512):
    m, d = x.shape
    n_groups, group, _ = w_grp.shape
    pool_width = n_groups * group
    n_pc = b_pc.shape[1]
    conv_width = (n_pc - pool_width) // 2
    assert max(POOL_WINDOWS) <= HALO and seq % tm == 0 and tm % HALO == 0
    halo_blocks = tm // HALO

    return pl.pallas_call(
        functools.partial(_mix_in_kernel, tiles_per_seq=seq // tm),
        grid=(m // tm,),
        in_specs=[
            pl.BlockSpec((tm, d), lambda i: (i, 0)),
            pl.BlockSpec((HALO, d), lambda i: (jnp.maximum(i * halo_blocks - 1, 0), 0)),
            _resident((1, d)),
            _resident((d, n_pc)),
            _resident(b_pc.shape),
            _resident(w_grp.shape),
            _resident((1, pool_width)),
        ],
        out_specs=[
            pl.BlockSpec((tm, pool_width), lambda i: (i, 0)),
            pl.BlockSpec((tm, conv_width), lambda i: (i, 0)),
        ],
        out_shape=[
            jax.ShapeDtypeStruct((m, pool_width), _BF16),
            jax.ShapeDtypeStruct((m, conv_width), _F32),
        ],
        scratch_shapes=[
            pltpu.VMEM((HALO + tm, d), _BF16),
            pltpu.VMEM((HALO + tm, pool_width), _F32),
            pltpu.VMEM((HALO + tm, pool_width), _F32),
            pltpu.VMEM((HALO + tm, pool_width), _F32),
        ],
        compiler_params=pltpu.CompilerParams(
            dimension_semantics=("parallel",), vmem_limit_bytes=VMEM_LIMIT_BYTES),
        name="mix_in",
    )(x, x, g, w_in, b_pc, w_grp, pool_scale)


def _mix_out_kernel(x_ref, g_ref, wgate_refs, bgate_ref, mixed_ref, glu_ref, ghalo_ref, dw_ref,
                    cb_ref, lng_ref, lnb_ref, pp_ref, cp_ref, wo_ref, o_ref,
                    m_ref, gl_ref, conv_ref, c_ref, ga_ref, gb_ref, h_ref,
                    *, tiles_per_seq, col_chunk, ln_rows):
    i = pl.program_id(0)
    tm, d = x_ref.shape
    conv_width = glu_ref.shape[1]
    n_lane_blocks = conv_width // LANES
    n_taps = dw_ref.shape[0]
    first_in_seq = (i % tiles_per_seq) == 0

    for cb in range(n_lane_blocks):
        cs = slice(cb * LANES, (cb + 1) * LANES)
        gl_ref[cb, 0:HALO, :] = jnp.where(first_in_seq, 0.0, ghalo_ref[:, cs])
        gl_ref[cb, HALO:, :] = glu_ref[:, cs]

    rows_per_group = min(tm, CONV_ACCUMULATORS * SUBLANES)

    def conv_lane_block(cb):
        bias = cb_ref[pl.ds(cb, SUBLANES, stride=0), :]
        for r0 in range(0, tm, rows_per_group):
            starts = [r0 + blk * SUBLANES * ROW_STRIDE + ph
                      for blk in range(rows_per_group // (SUBLANES * ROW_STRIDE))
                      for ph in range(ROW_STRIDE)]
            accs = [bias for _ in starts]
            for t in range(n_taps):
                w = dw_ref[t, pl.ds(cb, SUBLANES, stride=0), :]
                off = HALO - (n_taps - 1) + t
                accs = [acc + w * gl_ref[cb, pl.ds(s + off, SUBLANES, stride=ROW_STRIDE), :]
                        for acc, s in zip(accs, starts)]
            for acc, s in zip(accs, starts):
                conv_ref[cb, pl.ds(s, SUBLANES, stride=ROW_STRIDE), :] = acc

    h_ref[...] = _rmsnorm(x_ref[...], g_ref[...]).astype(_BF16)
    gate_block = wgate_refs[0].shape[1]

    def gate_weight(col):
        ref, lo = wgate_refs[col // gate_block], col % gate_block
        return ref[:, lo:lo + col_chunk]

    n_chunks = d // col_chunk
    convs_per_chunk = n_lane_blocks // n_chunks
    for n in range(n_chunks):
        ca = slice(n * col_chunk, (n + 1) * col_chunk)
        cb = slice(d + n * col_chunk, d + (n + 1) * col_chunk)
        for k in range(convs_per_chunk):
            conv_lane_block(n * convs_per_chunk + k)
        h = h_ref[...]
        gate_a = jax.nn.sigmoid(_dot(h, gate_weight(ca.start)) + bgate_ref[:, ca])
        ga_ref[:, ca] = gate_a * _dot(mixed_ref[...], pp_ref[:, ca])
        gb_ref[:, ca] = jax.nn.sigmoid(_dot(h, gate_weight(cb.start)) + bgate_ref[:, cb])

    for r0 in range(0, tm, ln_rows):
        y = [conv_ref[cb, r0:r0 + ln_rows, :] for cb in range(n_lane_blocks)]
        mu = jnp.sum(sum(y), axis=-1, keepdims=True) * (1.0 / conv_width)
        yc = [v - mu for v in y]
        var = jnp.sum(sum(v * v for v in yc), axis=-1, keepdims=True) * (1.0 / conv_width)
        inv = lax.rsqrt(var + EPS)
        for cb in range(n_lane_blocks):
            cs = slice(cb * LANES, (cb + 1) * LANES)
            z = (yc[cb] * inv) * lng_ref[:, cs] + lnb_ref[:, cs]
            c_ref[r0:r0 + ln_rows, cs] = (z * jax.nn.sigmoid(z)).astype(_BF16)

    c = c_ref[...]
    for n in range(n_chunks):
        ca = slice(n * col_chunk, (n + 1) * col_chunk)
        bb = _dot(c, cp_ref[:, ca])
        m_ref[:, ca] = (ga_ref[:, ca] + gb_ref[:, ca] * bb).astype(_BF16)
    o_ref[...] = x_ref[...] + _dot(m_ref[...], wo_ref[...])


def _mix_out(x, g, w_in, gate_col, b_gate, mixed, glu, conv_dw, conv_b, ln_g, ln_b, pool_w_proj,
             conv_w_proj, w_out, *, seq, tm=256):
    m, d = x.shape
    pool_width = mixed.shape[1]
    n_taps, conv_width = conv_dw.shape
    assert n_taps - 1 <= HALO and seq % tm == 0 and tm % (SUBLANES * ROW_STRIDE) == 0
    halo_blocks = tm // HALO
    n_lane_blocks = conv_width // LANES
    col_chunk = 512
    gate_block = math.gcd(gate_col, d)
    n_gate_blocks = 2 * d // gate_block
    assert gate_block % col_chunk == 0 and gate_col + 2 * d == w_in.shape[1]
    assert d % col_chunk == 0 and n_lane_blocks == 2 * (d // col_chunk)

    return pl.pallas_call(
        functools.partial(_mix_out_kernel, tiles_per_seq=seq // tm, col_chunk=col_chunk, ln_rows=32),
        grid=(m // tm,),
        in_specs=[
            pl.BlockSpec((tm, d), lambda i: (i, 0)),
            _resident((1, d)),
            [pl.BlockSpec((d, gate_block), lambda i, k=k: (0, gate_col // gate_block + k),
                          pipeline_mode=pl.Buffered(1)) for k in range(n_gate_blocks)],
            _resident(b_gate.shape),
            pl.BlockSpec((tm, pool_width), lambda i: (i, 0)),
            pl.BlockSpec((tm, conv_width), lambda i: (i, 0)),
            pl.BlockSpec((HALO, conv_width), lambda i: (jnp.maximum(i * halo_blocks - 1, 0), 0)),
            _resident((n_taps, n_lane_blocks, LANES)),
            _resident((n_lane_blocks, LANES)),
            _resident((1, conv_width)),
            _resident((1, conv_width)),
            _resident(pool_w_proj.shape),
            _resident(conv_w_proj.shape),
            _resident(w_out.shape),
        ],
        out_specs=pl.BlockSpec((tm, d), lambda i: (i, 0)),
        out_shape=jax.ShapeDtypeStruct((m, d), _F32),
        scratch_shapes=[
            pltpu.VMEM((tm, d), _BF16),
            pltpu.VMEM((n_lane_blocks, HALO + tm, LANES), _F32),
            pltpu.VMEM((n_lane_blocks, tm, LANES), _F32),
            pltpu.VMEM((tm, conv_width), _BF16),
            pltpu.VMEM((tm, d), _F32),
            pltpu.VMEM((tm, d), _F32),
            pltpu.VMEM((tm, d), _BF16),
        ],
        compiler_params=pltpu.CompilerParams(
            dimension_semantics=("parallel",), vmem_limit_bytes=VMEM_LIMIT_BYTES),
        name="mix_out",
    )(x, g, [w_in] * n_gate_blocks, b_gate, mixed, glu, glu, conv_dw.reshape(n_taps, n_lane_blocks, LANES),
      conv_b.reshape(n_lane_blocks, LANES), ln_g, ln_b, pool_w_proj, conv_w_proj, w_out)


def kernel(x, ffn1_norm, ffn1_w_in, ffn1_w_out, mix_norm, w_in, b_in, pool_w_grp, pool_scale,
           pool_w_proj, conv_dw, conv_b, conv_ln_g, conv_ln_b, conv_w_proj, w_out, ffn2_norm,
           ffn2_w_in, ffn2_w_out, final_norm):
    batch, seq, d = x.shape
    depth = ffn1_norm.shape[0]
    if depth == 0:
        raise ValueError("depth must be positive")
    pool_width = pool_w_proj.shape[1]
    conv_width = conv_w_proj.shape[1]
    n_pc = pool_width + 2 * conv_width

    def row(v):
        return v.reshape(1, -1).astype(_F32)

    xf = x.reshape(batch * seq, d)
    for l in range(depth):
        xf = _ffn(xf, row(ffn1_norm[l]), ffn1_w_in[l], ffn1_w_out[l])
        w_in_bf16 = w_in[l].astype(_BF16)
        mixed, glu = _mix_in(
            xf, row(mix_norm[l]), w_in_bf16, row(b_in[l][:n_pc]),
            pool_w_grp[l].astype(_BF16), row(pool_scale[l]), seq=seq)
        xf = _mix_out(
            xf, row(mix_norm[l]), w_in_bf16, n_pc, row(b_in[l][n_pc:]), mixed, glu,
            conv_dw[l], conv_b[l], row(conv_ln_g[l]), row(conv_ln_b[l]),
            pool_w_proj[l].astype(_BF16), conv_w_proj[l].astype(_BF16), w_out[l].astype(_BF16),
            seq=seq)
        final_g = row(final_norm) if l == depth - 1 else None
        xf = _ffn(xf, row(ffn2_norm[l]), ffn2_w_in[l], ffn2_w_out[l], final_g)
    return xf.reshape(batch, seq, d)
```

```python
import functools
import math

import jax
import jax.numpy as jnp
from jax import lax
from jax.experimental import pallas as pl
from jax.experimental.pallas import tpu as pltpu

EPS = 1e-6
FFN_RESIDUAL_WEIGHT = 0.5
POOL_WINDOWS = (2, 4, 8, 16)

HALO = 32
SUBLANES = 8
LANES = 128
ROW_STRIDE = 4
CONV_ACCUMULATORS = 8
VMEM_LIMIT_BYTES = 58 * 1024 * 1024

_F32 = jnp.float32
_BF16 = jnp.bfloat16


def _rmsnorm(x, g):
    ms = jnp.mean(x * x, axis=-1, keepdims=True)
    return (x * lax.rsqrt(ms + EPS)) * g


def _dot(a, b):
    return jnp.dot(a, b, preferred_element_type=_F32)


def _resident(shape):
    return pl.BlockSpec(shape, lambda *_: (0,) * len(shape), pipeline_mode=pl.Buffered(1))


def _ffn_body(j, n_j, x_ref, g_ref, fg_ref, wg_ref, wu_ref, wo_ref, o_ref, h_ref,
              *, row_chunk, col_chunk, first_rows):
    tm, d = x_ref.shape
    n_row_chunks = tm // row_chunk

    def chunk_update(h, rows, accumulate):
        gate = _dot(h, wg_ref[...])
        up = _dot(h, wu_ref[...])
        act = ((gate * jax.nn.sigmoid(gate)) * up).astype(_BF16)
        for n in range(d // col_chunk):
            cs = slice(n * col_chunk, (n + 1) * col_chunk)
            y = _dot(act, wo_ref[:, cs])
            o_ref[rows, cs] = o_ref[rows, cs] + y if accumulate else y

    @pl.when(j == 0)
    def _():
        for r0 in range(0, tm, first_rows):
            rows = slice(r0, r0 + first_rows)
            h = _rmsnorm(x_ref[rows, :], g_ref[...]).astype(_BF16)
            h_ref[rows, :] = h
            chunk_update(h, rows, accumulate=False)

    @pl.when(j > 0)
    def _():
        chunk_update(h_ref[...], slice(None), accumulate=True)

    @pl.when(j == n_j - 1)
    def _():
        def body(r, carry):
            r0 = pl.multiple_of(r * row_chunk, row_chunk)
            rows = pl.ds(r0, row_chunk)
            y = x_ref[rows, :] + FFN_RESIDUAL_WEIGHT * o_ref[rows, :]
            if fg_ref is not None:
                y = _rmsnorm(y, fg_ref[...])
            o_ref[rows, :] = y
            return carry

        lax.fori_loop(0, n_row_chunks, body, 0)


def _ffn_head_kernel(x_ref, g_ref, wg32_ref, wu32_ref, wo32_ref, *rest, final_norm, **chunks):
    fg_ref = rest[0] if final_norm else None
    o_ref, wg_ref, wu_ref, wo_ref, h_ref = rest[1:] if final_norm else rest
    wg_ref[...] = wg32_ref[...].astype(_BF16)
    wu_ref[...] = wu32_ref[...].astype(_BF16)
    wo_ref[...] = wo32_ref[...].astype(_BF16)
    _ffn_body(pl.program_id(0), pl.num_programs(0), x_ref, g_ref, fg_ref, wg_ref, wu_ref, wo_ref,
              o_ref, h_ref, **chunks)


def _ffn_tail_kernel(x_ref, y0_ref, g_ref, wg_ref, wu_ref, wo_ref, *rest, final_norm, **chunks):
    fg_ref = rest[0] if final_norm else None
    o_ref, h_ref = rest[1:] if final_norm else rest
    i, j = pl.program_id(0), pl.program_id(1)
    copy_cols = y0_ref.shape[1]

    @pl.when(i == 0)
    def _():
        for k in range(o_ref.shape[1] // copy_cols):
            @pl.when(j == k)
            def _(k=k):
                o_ref[:, k * copy_cols:(k + 1) * copy_cols] = y0_ref[...]

    @pl.when(i > 0)
    def _():
        _ffn_body(j, pl.num_programs(1), x_ref, g_ref, fg_ref, wg_ref, wu_ref, wo_ref, o_ref, h_ref,
                  **chunks)


def _ffn(x, g, w_in, w_out, final_g=None, *, tm=1024, tf=512, tf_head=256):
    m, d = x.shape
    f = w_out.shape[0]
    final_norm = final_g is not None
    chunks = dict(row_chunk=128, col_chunk=512, first_rows=256)
    extra_specs = [pl.BlockSpec((1, d), lambda *_: (0, 0))] if final_norm else []
    extra_args = [final_g] if final_norm else []
    suffix = "_final" if final_norm else ""

    n_head = f // tf_head
    y, wg, wu, wo = pl.pallas_call(
        functools.partial(_ffn_head_kernel, final_norm=final_norm, **chunks),
        grid=(n_head,),
        in_specs=[
            pl.BlockSpec((tm, d), lambda j: (0, 0), pipeline_mode=pl.Buffered(1)),
            pl.BlockSpec((1, d), lambda j: (0, 0)),
            pl.BlockSpec((d, tf_head), lambda j: (0, j)),
            pl.BlockSpec((d, tf_head), lambda j: (0, j + n_head)),
            pl.BlockSpec((tf_head, d), lambda j: (j, 0)),
        ] + extra_specs,
        out_specs=[
            pl.BlockSpec((tm, d), lambda j: (0, 0)),
            pl.BlockSpec((d, tf_head), lambda j: (0, j)),
            pl.BlockSpec((d, tf_head), lambda j: (0, j)),
            pl.BlockSpec((tf_head, d), lambda j: (j, 0)),
        ],
        out_shape=[
            jax.ShapeDtypeStruct((tm, d), _F32),
            jax.ShapeDtypeStruct((d, f), _BF16),
            jax.ShapeDtypeStruct((d, f), _BF16),
            jax.ShapeDtypeStruct((f, d), _BF16),
        ],
        scratch_shapes=[pltpu.VMEM((tm, d), _BF16)],
        compiler_params=pltpu.CompilerParams(
            dimension_semantics=("arbitrary",), vmem_limit_bytes=VMEM_LIMIT_BYTES),
        name="ffn_head" + suffix,
    )(x, g, w_in, w_in, w_out, *extra_args)

    n_f = f // tf
    copy_cols = 256
    n_copy = d // copy_cols
    assert n_copy <= n_f

    def chunk(i, j):
        return jnp.where(i == 0, 0, j)

    return pl.pallas_call(
        functools.partial(_ffn_tail_kernel, final_norm=final_norm, **chunks),
        grid=(m // tm, n_f),
        in_specs=[
            pl.BlockSpec((tm, d), lambda i, j: (i, 0)),
            pl.BlockSpec((tm, copy_cols),
                         lambda i, j: (0, jnp.where(i == 0, jnp.minimum(j, n_copy - 1), n_copy - 1))),
            pl.BlockSpec((1, d), lambda i, j: (0, 0)),
            pl.BlockSpec((d, tf), lambda i, j: (0, chunk(i, j))),
            pl.BlockSpec((d, tf), lambda i, j: (0, chunk(i, j))),
            pl.BlockSpec((tf, d), lambda i, j: (chunk(i, j), 0)),
        ] + extra_specs,
        out_specs=pl.BlockSpec((tm, d), lambda i, j: (i, 0)),
        out_shape=jax.ShapeDtypeStruct((m, d), _F32),
        scratch_shapes=[pltpu.VMEM((tm, d), _BF16)],
        compiler_params=pltpu.CompilerParams(
            dimension_semantics=("parallel", "arbitrary"), vmem_limit_bytes=VMEM_LIMIT_BYTES),
        name="ffn" + suffix,
    )(x, y, g, wg, wu, wo, *extra_args)


def _mix_in_kernel(x_ref, halo_ref, g_ref, w_ref, b_ref, wgrp_ref, pscale_ref, mixed_ref, glu_ref,
                   h_ref, u_ref, s_ref, t_ref, *, tiles_per_seq):
    i = pl.program_id(0)
    tm = x_ref.shape[0]
    pool_width = mixed_ref.shape[1]
    conv_width = glu_ref.shape[1]
    group = pool_width // len(POOL_WINDOWS)
    tile_in_seq = i % tiles_per_seq

    h_ref[0:HALO, :] = _rmsnorm(halo_ref[...], g_ref[...]).astype(_BF16)
    h_ref[HALO:HALO + tm, :] = _rmsnorm(x_ref[...], g_ref[...]).astype(_BF16)

    u = _dot(h_ref[...], w_ref[:, 0:pool_width]) + b_ref[:, 0:pool_width]
    u_ref[0:HALO, :] = jnp.where(tile_in_seq == 0, 0.0, u[0:HALO])
    u_ref[HALO:, :] = u[HALO:]

    pos = tile_in_seq * tm + lax.broadcasted_iota(jnp.int32, (tm, 1), 0)
    for gi, window in enumerate(POOL_WINDOWS):
        cs = slice(gi * group, (gi + 1) * group)
        src, shift, lo, level = u_ref, 1, 8, 0
        while shift < window:
            dst = (s_ref, t_ref)[level % 2]
            dst[lo:, cs] = src[lo:, cs] + src[lo - shift:HALO + tm - shift, cs]
            src, shift, lo, level = dst, shift * 2, lo + 8, level + 1
        inv_count = 1.0 / jnp.minimum(pos + 1, window).astype(_F32)
        pooled = src[HALO:, cs] * inv_count - u_ref[HALO:, cs]
        mixed = _dot(pooled.astype(_BF16), wgrp_ref[gi])
        mixed_ref[:, cs] = (mixed * pscale_ref[:, cs]).astype(_BF16)

    h = h_ref[HALO:, :]
    za = _dot(h, w_ref[:, pool_width:pool_width + conv_width]) \
        + b_ref[:, pool_width:pool_width + conv_width]
    zg = _dot(h, w_ref[:, pool_width + conv_width:]) + b_ref[:, pool_width + conv_width:]
    glu_ref[...] = za * jax.nn.sigmoid(zg)


def _mix_in(x, g, w_in, b_pc, w_grp, pool_scale, *, seq, tm=512):
    m, d = x.shape
    n_groups, group, _ = w_grp.shape
    pool_width = n_groups * group
    n_pc = b_pc.shape[1]
    conv_width = (n_pc - pool_width) // 2
    assert max(POOL_WINDOWS) <= HALO and seq % tm == 0 and tm % HALO == 0
    halo_blocks = tm // HALO

    return pl.pallas_call(
        functools.partial(_mix_in_kernel, tiles_per_seq=seq // tm),
        grid=(m // tm,),
        in_specs=[
            pl.BlockSpec((tm, d), lambda i: (i, 0)),
            pl.BlockSpec((HALO, d), lambda i: (jnp.maximum(i * halo_blocks - 1, 0), 0)),
            _resident((1, d)),
            _resident((d, n_pc)),
            _resident(b_pc.shape),
            _resident(w_grp.shape),
            _resident((1, pool_width)),
        ],
        out_specs=[
            pl.BlockSpec((tm, pool_width), lambda i: (i, 0)),
            pl.BlockSpec((tm, conv_width), lambda i: (i, 0)),
        ],
        out_shape=[
            jax.ShapeDtypeStruct((m, pool_width), _BF16),
            jax.ShapeDtypeStruct((m, conv_width), _F32),
        ],
        scratch_shapes=[
            pltpu.VMEM((HALO + tm, d), _BF16),
            pltpu.VMEM((HALO + tm, pool_width), _F32),
            pltpu.VMEM((HALO + tm, pool_width), _F32),
            pltpu.VMEM((HALO + tm, pool_width), _F32),
        ],
        compiler_params=pltpu.CompilerParams(
            dimension_semantics=("parallel",), vmem_limit_bytes=VMEM_LIMIT_BYTES),
        name="mix_in",
    )(x, x, g, w_in, b_pc, w_grp, pool_scale)


def _mix_out_kernel(x_ref, g_ref, wgate_refs, bgate_ref, mixed_ref, glu_ref, ghalo_ref, dw_ref,
                    cb_ref, lng_ref, lnb_ref, pp_ref, cp_ref, wo_ref, o_ref,
                    m_ref, gl_ref, conv_ref, c_ref, ga_ref, gb_ref, h_ref,
                    *, tiles_per_seq, col_chunk, ln_rows):
    i = pl.program_id(0)
    tm, d = x_ref.shape
    conv_width = glu_ref.shape[1]
    n_lane_blocks = conv_width // LANES
    n_taps = dw_ref.shape[0]
    first_in_seq = (i % tiles_per_seq) == 0

    for cb in range(n_lane_blocks):
        cs = slice(cb * LANES, (cb + 1) * LANES)
        gl_ref[cb, 0:HALO, :] = jnp.where(first_in_seq, 0.0, ghalo_ref[:, cs])
        gl_ref[cb, HALO:, :] = glu_ref[:, cs]

    rows_per_group = min(tm, CONV_ACCUMULATORS * SUBLANES)

    def conv_lane_block(cb):
        bias = cb_ref[pl.ds(cb, SUBLANES, stride=0), :]
        for r0 in range(0, tm, rows_per_group):
            starts = [r0 + blk * SUBLANES * ROW_STRIDE + ph
                      for blk in range(rows_per_group // (SUBLANES * ROW_STRIDE))
                      for ph in range(ROW_STRIDE)]
            accs = [bias for _ in starts]
            for t in range(n_taps):
                w = dw_ref[t, pl.ds(cb, SUBLANES, stride=0), :]
                off = HALO - (n_taps - 1) + t
                accs = [acc + w * gl_ref[cb, pl.ds(s + off, SUBLANES, stride=ROW_STRIDE), :]
                        for acc, s in zip(accs, starts)]
            for acc, s in zip(accs, starts):
                conv_ref[cb, pl.ds(s, SUBLANES, stride=ROW_STRIDE), :] = acc

    h_ref[...] = _rmsnorm(x_ref[...], g_ref[...]).astype(_BF16)
    gate_block = wgate_refs[0].shape[1]

    def gate_weight(col):
        ref, lo = wgate_refs[col // gate_block], col % gate_block
        return ref[:, lo:lo + col_chunk]

    n_chunks = d // col_chunk
    convs_per_chunk = n_lane_blocks // n_chunks
    for n in range(n_chunks):
        ca = slice(n * col_chunk, (n + 1) * col_chunk)
        cb = slice(d + n * col_chunk, d + (n + 1) * col_chunk)
        for k in range(convs_per_chunk):
            conv_lane_block(n * convs_per_chunk + k)
        h = h_ref[...]
        gate_a = jax.nn.sigmoid(_dot(h, gate_weight(ca.start)) + bgate_ref[:, ca])
        ga_ref[:, ca] = gate_a * _dot(mixed_ref[...], pp_ref[:, ca])
        gb_ref[:, ca] = jax.nn.sigmoid(_dot(h, gate_weight(cb.start)) + bgate_ref[:, cb])

    for r0 in range(0, tm, ln_rows):
        y = [conv_ref[cb, r0:r0 + ln_rows, :] for cb in range(n_lane_blocks)]
        mu = jnp.sum(sum(y), axis=-1, keepdims=True) * (1.0 / conv_width)
        yc = [v - mu for v in y]
        var = jnp.sum(sum(v * v for v in yc), axis=-1, keepdims=True) * (1.0 / conv_width)
        inv = lax.rsqrt(var + EPS)
        for cb in range(n_lane_blocks):
            cs = slice(cb * LANES, (cb + 1) * LANES)
            z = (yc[cb] * inv) * lng_ref[:, cs] + lnb_ref[:, cs]
            c_ref[r0:r0 + ln_rows, cs] = (z * jax.nn.sigmoid(z)).astype(_BF16)

    c = c_ref[...]
    for n in range(n_chunks):
        ca = slice(n * col_chunk, (n + 1) * col_chunk)
        bb = _dot(c, cp_ref[:, ca])
        m_ref[:, ca] = (ga_ref[:, ca] + gb_ref[:, ca] * bb).astype(_BF16)
    o_ref[...] = x_ref[...] + _dot(m_ref[...], wo_ref[...])


def _mix_out(x, g, w_in, gate_col, b_gate, mixed, glu, conv_dw, conv_b, ln_g, ln_b, pool_w_proj,
             conv_w_proj, w_out, *, seq, tm=256):
    m, d = x.shape
    pool_width = mixed.shape[1]
    n_taps, conv_width = conv_dw.shape
    assert n_taps - 1 <= HALO and seq % tm == 0 and tm % (SUBLANES * ROW_STRIDE) == 0
    halo_blocks = tm // HALO
    n_lane_blocks = conv_width // LANES
    col_chunk = 512
    gate_block = math.gcd(gate_col, d)
    n_gate_blocks = 2 * d // gate_block
    assert gate_block % col_chunk == 0 and gate_col + 2 * d == w_in.shape[1]
    assert d % col_chunk == 0 and n_lane_blocks == 2 * (d // col_chunk)

    return pl.pallas_call(
        functools.partial(_mix_out_kernel, tiles_per_seq=seq // tm, col_chunk=col_chunk, ln_rows=32),
        grid=(m // tm,),
        in_specs=[
            pl.BlockSpec((tm, d), lambda i: (i, 0)),
            _resident((1, d)),
            [pl.BlockSpec((d, gate_block), lambda i, k=k: (0, gate_col // gate_block + k),
                          pipeline_mode=pl.Buffered(1)) for k in range(n_gate_blocks)],
            _resident(b_gate.shape),
            pl.BlockSpec((tm, pool_width), lambda i: (i, 0)),
            pl.BlockSpec((tm, conv_width), lambda i: (i, 0)),
            pl.BlockSpec((HALO, conv_width), lambda i: (jnp.maximum(i * halo_blocks - 1, 0), 0)),
            _resident((n_taps, n_lane_blocks, LANES)),
            _resident((n_lane_blocks, LANES)),
            _resident((1, conv_width)),
            _resident((1, conv_width)),
            _resident(pool_w_proj.shape),
            _resident(conv_w_proj.shape),
            _resident(w_out.shape),
        ],
        out_specs=pl.BlockSpec((tm, d), lambda i: (i, 0)),
        out_shape=jax.ShapeDtypeStruct((m, d), _F32),
        scratch_shapes=[
            pltpu.VMEM((tm, d), _BF16),
            pltpu.VMEM((n_lane_blocks, HALO + tm, LANES), _F32),
            pltpu.VMEM((n_lane_blocks, tm, LANES), _F32),
            pltpu.VMEM((tm, conv_width), _BF16),
            pltpu.VMEM((tm, d), _F32),
            pltpu.VMEM((tm, d), _F32),
            pltpu.VMEM((tm, d), _BF16),
        ],
        compiler_params=pltpu.CompilerParams(
            dimension_semantics=("parallel",), vmem_limit_bytes=VMEM_LIMIT_BYTES),
        name="mix_out",
    )(x, g, [w_in] * n_gate_blocks, b_gate, mixed, glu, glu, conv_dw.reshape(n_taps, n_lane_blocks, LANES),
      conv_b.reshape(n_lane_blocks, LANES), ln_g, ln_b, pool_w_proj, conv_w_proj, w_out)


def kernel(x, ffn1_norm, ffn1_w_in, ffn1_w_out, mix_norm, w_in, b_in, pool_w_grp, pool_scale,
           pool_w_proj, conv_dw, conv_b, conv_ln_g, conv_ln_b, conv_w_proj, w_out, ffn2_norm,
           ffn2_w_in, ffn2_w_out, final_norm):
    batch, seq, d = x.shape
    depth = ffn1_norm.shape[0]
    if depth == 0:
        raise ValueError("depth must be positive")
    pool_width = pool_w_proj.shape[1]
    conv_width = conv_w_proj.shape[1]
    n_pc = pool_width + 2 * conv_width

    def row(v):
        return v.reshape(1, -1).astype(_F32)

    xf = x.reshape(batch * seq, d)
    for l in range(depth):
        xf = _ffn(xf, row(ffn1_norm[l]), ffn1_w_in[l], ffn1_w_out[l])
        w_in_bf16 = w_in[l].astype(_BF16)
        mixed, glu = _mix_in(
            xf, row(mix_norm[l]), w_in_bf16, row(b_in[l][:n_pc]),
            pool_w_grp[l].astype(_BF16), row(pool_scale[l]), seq=seq)
        xf = _mix_out(
            xf, row(mix_norm[l]), w_in_bf16, n_pc, row(b_in[l][n_pc:]), mixed, glu,
            conv_dw[l], conv_b[l], row(conv_ln_g[l]), row(conv_ln_b[l]),
            pool_w_proj[l].astype(_BF16), conv_w_proj[l].astype(_BF16), w_out[l].astype(_BF16),
            seq=seq)
        final_g = row(final_norm) if l == depth - 1 else None
        xf = _ffn(xf, row(ffn2_norm[l]), ffn2_w_in[l], ffn2_w_out[l], final_g)
    return xf.reshape(batch, seq, d)
```

```python
import functools
import math

import jax
import jax.numpy as jnp
from jax import lax
from jax.experimental import pallas as pl
from jax.experimental.pallas import tpu as pltpu

EPS = 1e-6
FFN_RESIDUAL_WEIGHT = 0.5
POOL_WINDOWS = (2, 4, 8, 16)

HALO = 32
SUBLANES = 8
LANES = 128
ROW_STRIDE = 4
CONV_ACCUMULATORS = 8
VMEM_LIMIT_BYTES = 58 * 1024 * 1024

_F32 = jnp.float32
_BF16 = jnp.bfloat16


def _rmsnorm(x, g):
    ms = jnp.mean(x * x, axis=-1, keepdims=True)
    return (x * lax.rsqrt(ms + EPS)) * g


def _dot(a, b):
    return jnp.dot(a, b, preferred_element_type=_F32)


def _resident(shape):
    return pl.BlockSpec(shape, lambda *_: (0,) * len(shape), pipeline_mode=pl.Buffered(1))


def _ffn_body(j, n_j, x_ref, g_ref, fg_ref, wg_ref, wu_ref, wo_ref, o_ref, h_ref,
              *, row_chunk, col_chunk, first_rows):
    tm, d = x_ref.shape
    n_row_chunks = tm // row_chunk

    def chunk_update(h, rows, first=False, last=False):
        gate = _dot(h, wg_ref[...])
        up = _dot(h, wu_ref[...])
        act = ((gate * jax.nn.sigmoid(gate)) * up).astype(_BF16)
        for n in range(d // col_chunk):
            cs = slice(n * col_chunk, (n + 1) * col_chunk)
            y = _dot(act, wo_ref[:, cs])
            if not first:
                y = o_ref[rows, cs] + y
            if last:
                y = x_ref[rows, cs] + FFN_RESIDUAL_WEIGHT * y
            o_ref[rows, cs] = y

    @pl.when(j == 0)
    def _():
        for r0 in range(0, tm, first_rows):
            rows = slice(r0, r0 + first_rows)
            h = _rmsnorm(x_ref[rows, :], g_ref[...]).astype(_BF16)
            h_ref[rows, :] = h
            chunk_update(h, rows, first=True)

    @pl.when(jnp.logical_and(j > 0, j < n_j - 1))
    def _():
        chunk_update(h_ref[...], slice(None))

    @pl.when(j == n_j - 1)
    def _():
        chunk_update(h_ref[...], slice(None), last=True)
        if fg_ref is not None:
            def body(r, carry):
                r0 = pl.multiple_of(r * row_chunk, row_chunk)
                rows = pl.ds(r0, row_chunk)
                o_ref[rows, :] = _rmsnorm(o_ref[rows, :], fg_ref[...])
                return carry

            lax.fori_loop(0, n_row_chunks, body, 0)


def _ffn_head_kernel(x_ref, g_ref, wg32_ref, wu32_ref, wo32_ref, *rest, final_norm, **chunks):
    fg_ref = rest[0] if final_norm else None
    o_ref, wg_ref, wu_ref, wo_ref, h_ref = rest[1:] if final_norm else rest
    wg_ref[...] = wg32_ref[...].astype(_BF16)
    wu_ref[...] = wu32_ref[...].astype(_BF16)
    wo_ref[...] = wo32_ref[...].astype(_BF16)
    _ffn_body(pl.program_id(0), pl.num_programs(0), x_ref, g_ref, fg_ref, wg_ref, wu_ref, wo_ref,
              o_ref, h_ref, **chunks)


def _ffn_tail_kernel(x_ref, y0_ref, g_ref, wg_ref, wu_ref, wo_ref, *rest, final_norm, **chunks):
    fg_ref = rest[0] if final_norm else None
    o_ref, h_ref = rest[1:] if final_norm else rest
    i, j = pl.program_id(0), pl.program_id(1)
    copy_cols = y0_ref.shape[1]

    @pl.when(i == 0)
    def _():
        for k in range(o_ref.shape[1] // copy_cols):
            @pl.when(j == k)
            def _(k=k):
                o_ref[:, k * copy_cols:(k + 1) * copy_cols] = y0_ref[...]

    @pl.when(i > 0)
    def _():
        _ffn_body(j, pl.num_programs(1), x_ref, g_ref, fg_ref, wg_ref, wu_ref, wo_ref, o_ref, h_ref,
                  **chunks)


def _ffn(x, g, w_in, w_out, final_g=None, *, tm=1024, tf=512, tf_head=256):
    m, d = x.shape
    f = w_out.shape[0]
    final_norm = final_g is not None
    chunks = dict(row_chunk=128, col_chunk=512, first_rows=256)
    extra_specs = [pl.BlockSpec((1, d), lambda *_: (0, 0))] if final_norm else []
    extra_args = [final_g] if final_norm else []
    suffix = "_final" if final_norm else ""

    n_head = f // tf_head
    assert f % tf == 0 and f % tf_head == 0 and f // tf >= 2 and n_head >= 2
    y, wg, wu, wo = pl.pallas_call(
        functools.partial(_ffn_head_kernel, final_norm=final_norm, **chunks),
        grid=(n_head,),
        in_specs=[
            pl.BlockSpec((tm, d), lambda j: (0, 0), pipeline_mode=pl.Buffered(1)),
            pl.BlockSpec((1, d), lambda j: (0, 0)),
            pl.BlockSpec((d, tf_head), lambda j: (0, j)),
            pl.BlockSpec((d, tf_head), lambda j: (0, j + n_head)),
            pl.BlockSpec((tf_head, d), lambda j: (j, 0)),
        ] + extra_specs,
        out_specs=[
            pl.BlockSpec((tm, d), lambda j: (0, 0)),
            pl.BlockSpec((d, tf_head), lambda j: (0, j)),
            pl.BlockSpec((d, tf_head), lambda j: (0, j)),
            pl.BlockSpec((tf_head, d), lambda j: (j, 0)),
        ],
        out_shape=[
            jax.ShapeDtypeStruct((tm, d), _F32),
            jax.ShapeDtypeStruct((d, f), _BF16),
            jax.ShapeDtypeStruct((d, f), _BF16),
            jax.ShapeDtypeStruct((f, d), _BF16),
        ],
        scratch_shapes=[pltpu.VMEM((tm, d), _BF16)],
        compiler_params=pltpu.CompilerParams(
            dimension_semantics=("arbitrary",), vmem_limit_bytes=VMEM_LIMIT_BYTES),
        name="ffn_head" + suffix,
    )(x, g, w_in, w_in, w_out, *extra_args)

    n_f = f // tf
    copy_cols = 256
    n_copy = d // copy_cols
    assert n_copy <= n_f

    def chunk(i, j):
        return jnp.where(i == 0, 0, j)

    return pl.pallas_call(
        functools.partial(_ffn_tail_kernel, final_norm=final_norm, **chunks),
        grid=(m // tm, n_f),
        in_specs=[
            pl.BlockSpec((tm, d), lambda i, j: (i, 0)),
            pl.BlockSpec((tm, copy_cols),
                         lambda i, j: (0, jnp.where(i == 0, jnp.minimum(j, n_copy - 1), n_copy - 1))),
            pl.BlockSpec((1, d), lambda i, j: (0, 0)),
            pl.BlockSpec((d, tf), lambda i, j: (0, chunk(i, j))),
            pl.BlockSpec((d, tf), lambda i, j: (0, chunk(i, j))),
            pl.BlockSpec((tf, d), lambda i, j: (chunk(i, j), 0)),
        ] + extra_specs,
        out_specs=pl.BlockSpec((tm, d), lambda i, j: (i, 0)),
        out_shape=jax.ShapeDtypeStruct((m, d), _F32),
        scratch_shapes=[pltpu.VMEM((tm, d), _BF16)],
        compiler_params=pltpu.CompilerParams(
            dimension_semantics=("parallel", "arbitrary"), vmem_limit_bytes=VMEM_LIMIT_BYTES),
        name="ffn" + suffix,
    )(x, y, g, wg, wu, wo, *extra_args)


def _mix_in_kernel(x_ref, halo_ref, g_ref, w_ref, b_ref, wgrp_ref, pscale_ref, mixed_ref, glu_ref,
                   h_ref, u_ref, s_ref, t_ref, *, tiles_per_seq):
    i = pl.program_id(0)
    tm = x_ref.shape[0]
    pool_width = mixed_ref.shape[1]
    conv_width = glu_ref.shape[1]
    group = pool_width // len(POOL_WINDOWS)
    tile_in_seq = i % tiles_per_seq

    h_ref[0:HALO, :] = _rmsnorm(halo_ref[...], g_ref[...]).astype(_BF16)
    h_ref[HALO:HALO + tm, :] = _rmsnorm(x_ref[...], g_ref[...]).astype(_BF16)

    u = _dot(h_ref[...], w_ref[:, 0:pool_width]) + b_ref[:, 0:pool_width]
    u_ref[0:HALO, :] = jnp.where(tile_in_seq == 0, 0.0, u[0:HALO])
    u_ref[HALO:, :] = u[HALO:]

    pos = tile_in_seq * tm + lax.broadcasted_iota(jnp.int32, (tm, 1), 0)
    for gi, window in enumerate(POOL_WINDOWS):
        cs = slice(gi * group, (gi + 1) * group)
        src, shift, lo, level = u_ref, 1, 8, 0
        while shift < window:
            dst = (s_ref, t_ref)[level % 2]
            dst[lo:, cs] = src[lo:, cs] + src[lo - shift:HALO + tm - shift, cs]
            src, shift, lo, level = dst, shift * 2, lo + 8, level + 1
        inv_count = 1.0 / jnp.minimum(pos + 1, window).astype(_F32)
        pooled = src[HALO:, cs] * inv_count - u_ref[HALO:, cs]
        mixed = _dot(pooled.astype(_BF16), wgrp_ref[gi])
        mixed_ref[:, cs] = (mixed * pscale_ref[:, cs]).astype(_BF16)

    h = h_ref[HALO:, :]
    za = _dot(h, w_ref[:, pool_width:pool_width + conv_width]) \
        + b_ref[:, pool_width:pool_width + conv_width]
    zg = _dot(h, w_ref[:, pool_width + conv_width:]) + b_ref[:, pool_width + conv_width:]
    glu_ref[...] = za * jax.nn.sigmoid(zg)


def _mix_in(x, g, w_in, b_pc, w_grp, pool_scale, *, seq, tm=512):
    m, d = x.shape
    n_groups, group, _ = w_grp.shape
    pool_width = n_groups * group
    n_pc = b_pc.shape[1]
    conv_width = (n_pc - pool_width) // 2
    assert max(POOL_WINDOWS) <= HALO and seq % tm == 0 and tm % HALO == 0
    halo_blocks = tm // HALO

    return pl.pallas_call(
        functools.partial(_mix_in_kernel, tiles_per_seq=seq // tm),
        grid=(m // tm,),
        in_specs=[
            pl.BlockSpec((tm, d), lambda i: (i, 0)),
            pl.BlockSpec((HALO, d), lambda i: (jnp.maximum(i * halo_blocks - 1, 0), 0)),
            _resident((1, d)),
            _resident((d, n_pc)),
            _resident(b_pc.shape),
            _resident(w_grp.shape),
            _resident((1, pool_width)),
        ],
        out_specs=[
            pl.BlockSpec((tm, pool_width), lambda i: (i, 0)),
            pl.BlockSpec((tm, conv_width), lambda i: (i, 0)),
        ],
        out_shape=[
            jax.ShapeDtypeStruct((m, pool_width), _BF16),
            jax.ShapeDtypeStruct((m, conv_width), _F32),
        ],
        scratch_shapes=[
            pltpu.VMEM((HALO + tm, d), _BF16),
            pltpu.VMEM((HALO + tm, pool_width), _F32),
            pltpu.VMEM((HALO + tm, pool_width), _F32),
            pltpu.VMEM((HALO + tm, pool_width), _F32),
        ],
        compiler_params=pltpu.CompilerParams(
            dimension_semantics=("parallel",), vmem_limit_bytes=VMEM_LIMIT_BYTES),
        name="mix_in",
    )(x, x, g, w_in, b_pc, w_grp, pool_scale)


def _mix_out_kernel(x_ref, g_ref, wgate_refs, bgate_ref, mixed_ref, glu_ref, ghalo_ref, dw_ref,
                    cb_ref, lng_ref, lnb_ref, pp_ref, cp_ref, wo_ref, o_ref,
                    m_ref, gl_ref, conv_ref, c_ref, ga_ref, gb_ref, h_ref,
                    *, tiles_per_seq, col_chunk, ln_rows):
    i = pl.program_id(0)
    tm, d = x_ref.shape
    conv_width = glu_ref.shape[1]
    n_lane_blocks = conv_width // LANES
    n_taps = dw_ref.shape[0]
    first_in_seq = (i % tiles_per_seq) == 0

    for cb in range(n_lane_blocks):
        cs = slice(cb * LANES, (cb + 1) * LANES)
        gl_ref[cb, 0:HALO, :] = jnp.where(first_in_seq, 0.0, ghalo_ref[:, cs])
        gl_ref[cb, HALO:, :] = glu_ref[:, cs]

    rows_per_group = min(tm, CONV_ACCUMULATORS * SUBLANES)

    def conv_lane_block(cb):
        bias = cb_ref[pl.ds(cb, SUBLANES, stride=0), :]
        for r0 in range(0, tm, rows_per_group):
            starts = [r0 + blk * SUBLANES * ROW_STRIDE + ph
                      for blk in range(rows_per_group // (SUBLANES * ROW_STRIDE))
                      for ph in range(ROW_STRIDE)]
            accs = [bias for _ in starts]
            for t in range(n_taps):
                w = dw_ref[t, pl.ds(cb, SUBLANES, stride=0), :]
                off = HALO - (n_taps - 1) + t
                accs = [acc + w * gl_ref[cb, pl.ds(s + off, SUBLANES, stride=ROW_STRIDE), :]
                        for acc, s in zip(accs, starts)]
            for acc, s in zip(accs, starts):
                conv_ref[cb, pl.ds(s, SUBLANES, stride=ROW_STRIDE), :] = acc

    h_ref[...] = _rmsnorm(x_ref[...], g_ref[...]).astype(_BF16)
    gate_block = wgate_refs[0].shape[1]

    def gate_weight(col):
        ref, lo = wgate_refs[col // gate_block], col % gate_block
        return ref[:, lo:lo + col_chunk]

    n_chunks = d // col_chunk
    convs_per_chunk = n_lane_blocks // n_chunks
    for n in range(n_chunks):
        ca = slice(n * col_chunk, (n + 1) * col_chunk)
        cb = slice(d + n * col_chunk, d + (n + 1) * col_chunk)
        for k in range(convs_per_chunk):
            conv_lane_block(n * convs_per_chunk + k)
        h = h_ref[...]
        gate_a = jax.nn.sigmoid(_dot(h, gate_weight(ca.start)) + bgate_ref[:, ca])
        ga_ref[:, ca] = gate_a * _dot(mixed_ref[...], pp_ref[:, ca])
        gb_ref[:, ca] = jax.nn.sigmoid(_dot(h, gate_weight(cb.start)) + bgate_ref[:, cb])

    for r0 in range(0, tm, ln_rows):
        y = [conv_ref[cb, r0:r0 + ln_rows, :] for cb in range(n_lane_blocks)]
        mu = jnp.sum(sum(y), axis=-1, keepdims=True) * (1.0 / conv_width)
        yc = [v - mu for v in y]
        var = jnp.sum(sum(v * v for v in yc), axis=-1, keepdims=True) * (1.0 / conv_width)
        inv = lax.rsqrt(var + EPS)
        for cb in range(n_lane_blocks):
            cs = slice(cb * LANES, (cb + 1) * LANES)
            z = (yc[cb] * inv) * lng_ref[:, cs] + lnb_ref[:, cs]
            c_ref[r0:r0 + ln_rows, cs] = (z * jax.nn.sigmoid(z)).astype(_BF16)

    c = c_ref[...]
    for n in range(n_chunks):
        ca = slice(n * col_chunk, (n + 1) * col_chunk)
        bb = _dot(c, cp_ref[:, ca])
        m_ref[:, ca] = (ga_ref[:, ca] + gb_ref[:, ca] * bb).astype(_BF16)
    o_ref[...] = x_ref[...] + _dot(m_ref[...], wo_ref[...])


def _mix_out(x, g, w_in, gate_col, b_gate, mixed, glu, conv_dw, conv_b, ln_g, ln_b, pool_w_proj,
             conv_w_proj, w_out, *, seq, tm=256):
    m, d = x.shape
    pool_width = mixed.shape[1]
    n_taps, conv_width = conv_dw.shape
    assert n_taps - 1 <= HALO and seq % tm == 0 and tm % (SUBLANES * ROW_STRIDE) == 0
    halo_blocks = tm // HALO
    n_lane_blocks = conv_width // LANES
    col_chunk = 512
    gate_block = math.gcd(gate_col, d)
    n_gate_blocks = 2 * d // gate_block
    assert gate_block % col_chunk == 0 and gate_col + 2 * d == w_in.shape[1]
    assert d % col_chunk == 0 and n_lane_blocks == 2 * (d // col_chunk)

    return pl.pallas_call(
        functools.partial(_mix_out_kernel, tiles_per_seq=seq // tm, col_chunk=col_chunk, ln_rows=32),
        grid=(m // tm,),
        in_specs=[
            pl.BlockSpec((tm, d), lambda i: (i, 0)),
            _resident((1, d)),
            [pl.BlockSpec((d, gate_block), lambda i, k=k: (0, gate_col // gate_block + k),
                          pipeline_mode=pl.Buffered(1)) for k in range(n_gate_blocks)],
            _resident(b_gate.shape),
            pl.BlockSpec((tm, pool_width), lambda i: (i, 0)),
            pl.BlockSpec((tm, conv_width), lambda i: (i, 0)),
            pl.BlockSpec((HALO, conv_width), lambda i: (jnp.maximum(i * halo_blocks - 1, 0), 0)),
            _resident((n_taps, n_lane_blocks, LANES)),
            _resident((n_lane_blocks, LANES)),
            _resident((1, conv_width)),
            _resident((1, conv_width)),
            _resident(pool_w_proj.shape),
            _resident(conv_w_proj.shape),
            _resident(w_out.shape),
        ],
        out_specs=pl.BlockSpec((tm, d), lambda i: (i, 0)),
        out_shape=jax.ShapeDtypeStruct((m, d), _F32),
        scratch_shapes=[
            pltpu.VMEM((tm, d), _BF16),
            pltpu.VMEM((n_lane_blocks, HALO + tm, LANES), _F32),
            pltpu.VMEM((n_lane_blocks, tm, LANES), _F32),
            pltpu.VMEM((tm, conv_width), _BF16),
            pltpu.VMEM((tm, d), _F32),
            pltpu.VMEM((tm, d), _F32),
            pltpu.VMEM((tm, d), _BF16),
        ],
        compiler_params=pltpu.CompilerParams(
            dimension_semantics=("parallel",), vmem_limit_bytes=VMEM_LIMIT_BYTES),
        name="mix_out",
    )(x, g, [w_in] * n_gate_blocks, b_gate, mixed, glu, glu, conv_dw.reshape(n_taps, n_lane_blocks, LANES),
      conv_b.reshape(n_lane_blocks, LANES), ln_g, ln_b, pool_w_proj, conv_w_proj, w_out)


def kernel(x, ffn1_norm, ffn1_w_in, ffn1_w_out, mix_norm, w_in, b_in, pool_w_grp, pool_scale,
           pool_w_proj, conv_dw, conv_b, conv_ln_g, conv_ln_b, conv_w_proj, w_out, ffn2_norm,
           ffn2_w_in, ffn2_w_out, final_norm):
    batch, seq, d = x.shape
    depth = ffn1_norm.shape[0]
    if depth == 0:
        raise ValueError("depth must be positive")
    pool_width = pool_w_proj.shape[1]
    conv_width = conv_w_proj.shape[1]
    n_pc = pool_width + 2 * conv_width

    def row(v):
        return v.reshape(1, -1).astype(_F32)

    xf = x.reshape(batch * seq, d)
    for l in range(depth):
        xf = _ffn(xf, row(ffn1_norm[l]), ffn1_w_in[l], ffn1_w_out[l])
        w_in_bf16 = w_in[l].astype(_BF16)
        mixed, glu = _mix_in(
            xf, row(mix_norm[l]), w_in_bf16, row(b_in[l][:n_pc]),
            pool_w_grp[l].astype(_BF16), row(pool_scale[l]), seq=seq)
        xf = _mix_out(
            xf, row(mix_norm[l]), w_in_bf16, n_pc, row(b_in[l][n_pc:]), mixed, glu,
            conv_dw[l], conv_b[l], row(conv_ln_g[l]), row(conv_ln_b[l]),
            pool_w_proj[l].astype(_BF16), conv_w_proj[l].astype(_BF16), w_out[l].astype(_BF16),
            seq=seq)
        final_g = row(final_norm) if l == depth - 1 else None
        xf = _ffn(xf, row(ffn2_norm[l]), ffn2_w_in[l], ffn2_w_out[l], final_g)
    return xf.reshape(batch, seq, d)
```

```python
import functools
import math

import jax
import jax.numpy as jnp
from jax import lax
from jax.experimental import pallas as pl
from jax.experimental.pallas import tpu as pltpu

EPS = 1e-6
FFN_RESIDUAL_WEIGHT = 0.5
POOL_WINDOWS = (2, 4, 8, 16)

HALO = 32
SUBLANES = 8
LANES = 128
ROW_STRIDE = 4
CONV_ACCUMULATORS = 16
VMEM_LIMIT_BYTES = 58 * 1024 * 1024

_F32 = jnp.float32
_BF16 = jnp.bfloat16


def _rmsnorm(x, g):
    ms = jnp.mean(x * x, axis=-1, keepdims=True)
    return (x * lax.rsqrt(ms + EPS)) * g


def _dot(a, b):
    return jnp.dot(a, b, preferred_element_type=_F32)


def _resident(shape):
    return pl.BlockSpec(shape, lambda *_: (0,) * len(shape), pipeline_mode=pl.Buffered(1))


def _ffn_body(j, n_j, x_ref, g_ref, fg_ref, wg_ref, wu_ref, wo_ref, o_ref, h_ref,
              *, row_chunk, col_chunk, first_rows):
    tm, d = x_ref.shape
    n_row_chunks = tm // row_chunk

    def chunk_update(h, rows, first=False, last=False):
        gate = _dot(h, wg_ref[...])
        up = _dot(h, wu_ref[...])
        act = ((gate * jax.nn.sigmoid(gate)) * up).astype(_BF16)
        for n in range(d // col_chunk):
            cs = slice(n * col_chunk, (n + 1) * col_chunk)
            y = _dot(act, wo_ref[:, cs])
            if not first:
                y = o_ref[rows, cs] + y
            if last:
                y = x_ref[rows, cs] + FFN_RESIDUAL_WEIGHT * y
            o_ref[rows, cs] = y

    @pl.when(j == 0)
    def _():
        for r0 in range(0, tm, first_rows):
            rows = slice(r0, r0 + first_rows)
            h = _rmsnorm(x_ref[rows, :], g_ref[...]).astype(_BF16)
            h_ref[rows, :] = h
            chunk_update(h, rows, first=True)

    @pl.when(jnp.logical_and(j > 0, j < n_j - 1))
    def _():
        chunk_update(h_ref[...], slice(None))

    @pl.when(j == n_j - 1)
    def _():
        chunk_update(h_ref[...], slice(None), last=True)
        if fg_ref is not None:
            def body(r, carry):
                r0 = pl.multiple_of(r * row_chunk, row_chunk)
                rows = pl.ds(r0, row_chunk)
                o_ref[rows, :] = _rmsnorm(o_ref[rows, :], fg_ref[...])
                return carry

            lax.fori_loop(0, n_row_chunks, body, 0)


def _ffn_head_kernel(x_ref, g_ref, wg32_ref, wu32_ref, wo32_ref, *rest, final_norm, **chunks):
    fg_ref = rest[0] if final_norm else None
    o_ref, wg_ref, wu_ref, wo_ref, h_ref = rest[1:] if final_norm else rest
    wg_ref[...] = wg32_ref[...].astype(_BF16)
    wu_ref[...] = wu32_ref[...].astype(_BF16)
    wo_ref[...] = wo32_ref[...].astype(_BF16)
    _ffn_body(pl.program_id(0), pl.num_programs(0), x_ref, g_ref, fg_ref, wg_ref, wu_ref, wo_ref,
              o_ref, h_ref, **chunks)


def _ffn_tail_kernel(x_ref, y0_ref, g_ref, wg_ref, wu_ref, wo_ref, *rest, final_norm, **chunks):
    fg_ref = rest[0] if final_norm else None
    o_ref, h_ref = rest[1:] if final_norm else rest
    i, j = pl.program_id(0), pl.program_id(1)
    copy_cols = y0_ref.shape[1]

    @pl.when(i == 0)
    def _():
        for k in range(o_ref.shape[1] // copy_cols):
            @pl.when(j == k)
            def _(k=k):
                o_ref[:, k * copy_cols:(k + 1) * copy_cols] = y0_ref[...]

    @pl.when(i > 0)
    def _():
        _ffn_body(j, pl.num_programs(1), x_ref, g_ref, fg_ref, wg_ref, wu_ref, wo_ref, o_ref, h_ref,
                  **chunks)


def _ffn(x, g, w_in, w_out, final_g=None, *, tm=1024, tf=512, tf_head=256):
    m, d = x.shape
    f = w_out.shape[0]
    final_norm = final_g is not None
    chunks = dict(row_chunk=128, col_chunk=512, first_rows=256)
    extra_specs = [pl.BlockSpec((1, d), lambda *_: (0, 0))] if final_norm else []
    extra_args = [final_g] if final_norm else []
    suffix = "_final" if final_norm else ""

    n_head = f // tf_head
    assert f % tf == 0 and f % tf_head == 0 and f // tf >= 2 and n_head >= 2
    y, wg, wu, wo = pl.pallas_call(
        functools.partial(_ffn_head_kernel, final_norm=final_norm, **chunks),
        grid=(n_head,),
        in_specs=[
            pl.BlockSpec((tm, d), lambda j: (0, 0), pipeline_mode=pl.Buffered(1)),
            pl.BlockSpec((1, d), lambda j: (0, 0)),
            pl.BlockSpec((d, tf_head), lambda j: (0, j)),
            pl.BlockSpec((d, tf_head), lambda j: (0, j + n_head)),
            pl.BlockSpec((tf_head, d), lambda j: (j, 0)),
        ] + extra_specs,
        out_specs=[
            pl.BlockSpec((tm, d), lambda j: (0, 0)),
            pl.BlockSpec((d, tf_head), lambda j: (0, j)),
            pl.BlockSpec((d, tf_head), lambda j: (0, j)),
            pl.BlockSpec((tf_head, d), lambda j: (j, 0)),
        ],
        out_shape=[
            jax.ShapeDtypeStruct((tm, d), _F32),
            jax.ShapeDtypeStruct((d, f), _BF16),
            jax.ShapeDtypeStruct((d, f), _BF16),
            jax.ShapeDtypeStruct((f, d), _BF16),
        ],
        scratch_shapes=[pltpu.VMEM((tm, d), _BF16)],
        compiler_params=pltpu.CompilerParams(
            dimension_semantics=("arbitrary",), vmem_limit_bytes=VMEM_LIMIT_BYTES),
        name="ffn_head" + suffix,
    )(x, g, w_in, w_in, w_out, *extra_args)

    n_f = f // tf
    copy_cols = 256
    n_copy = d // copy_cols
    assert n_copy <= n_f

    def chunk(i, j):
        return jnp.where(i == 0, 0, j)

    return pl.pallas_call(
        functools.partial(_ffn_tail_kernel, final_norm=final_norm, **chunks),
        grid=(m // tm, n_f),
        in_specs=[
            pl.BlockSpec((tm, d), lambda i, j: (i, 0)),
            pl.BlockSpec((tm, copy_cols),
                         lambda i, j: (0, jnp.where(i == 0, jnp.minimum(j, n_copy - 1), n_copy - 1))),
            pl.BlockSpec((1, d), lambda i, j: (0, 0)),
            pl.BlockSpec((d, tf), lambda i, j: (0, chunk(i, j))),
            pl.BlockSpec((d, tf), lambda i, j: (0, chunk(i, j))),
            pl.BlockSpec((tf, d), lambda i, j: (chunk(i, j), 0)),
        ] + extra_specs,
        out_specs=pl.BlockSpec((tm, d), lambda i, j: (i, 0)),
        out_shape=jax.ShapeDtypeStruct((m, d), _F32),
        scratch_shapes=[pltpu.VMEM((tm, d), _BF16)],
        compiler_params=pltpu.CompilerParams(
            dimension_semantics=("parallel", "arbitrary"), vmem_limit_bytes=VMEM_LIMIT_BYTES),
        name="ffn" + suffix,
    )(x, y, g, wg, wu, wo, *extra_args)


def _mix_in_kernel(x_ref, halo_ref, g_ref, w_ref, b_ref, wgrp_ref, pscale_ref, mixed_ref, glu_ref,
                   h_ref, u_ref, s_ref, t_ref, *, tiles_per_seq):
    i = pl.program_id(0)
    tm = x_ref.shape[0]
    pool_width = mixed_ref.shape[1]
    conv_width = glu_ref.shape[1]
    group = pool_width // len(POOL_WINDOWS)
    tile_in_seq = i % tiles_per_seq

    h_ref[0:HALO, :] = _rmsnorm(halo_ref[...], g_ref[...]).astype(_BF16)
    h_ref[HALO:HALO + tm, :] = _rmsnorm(x_ref[...], g_ref[...]).astype(_BF16)

    u = _dot(h_ref[...], w_ref[:, 0:pool_width]) + b_ref[:, 0:pool_width]
    u_ref[0:HALO, :] = jnp.where(tile_in_seq == 0, 0.0, u[0:HALO])
    u_ref[HALO:, :] = u[HALO:]

    pos = tile_in_seq * tm + lax.broadcasted_iota(jnp.int32, (tm, 1), 0)
    for gi, window in enumerate(POOL_WINDOWS):
        cs = slice(gi * group, (gi + 1) * group)
        src, shift, lo, level = u_ref, 1, 8, 0
        while shift < window:
            dst = (s_ref, t_ref)[level % 2]
            dst[lo:, cs] = src[lo:, cs] + src[lo - shift:HALO + tm - shift, cs]
            src, shift, lo, level = dst, shift * 2, lo + 8, level + 1
        inv_count = 1.0 / jnp.minimum(pos + 1, window).astype(_F32)
        pooled = src[HALO:, cs] * inv_count - u_ref[HALO:, cs]
        mixed = _dot(pooled.astype(_BF16), wgrp_ref[gi])
        mixed_ref[:, cs] = (mixed * pscale_ref[:, cs]).astype(_BF16)

    h = h_ref[HALO:, :]
    za = _dot(h, w_ref[:, pool_width:pool_width + conv_width]) \
        + b_ref[:, pool_width:pool_width + conv_width]
    zg = _dot(h, w_ref[:, pool_width + conv_width:]) + b_ref[:, pool_width + conv_width:]
    glu_ref[...] = za * jax.nn.sigmoid(zg)


def _mix_in(x, g, w_in, b_pc, w_grp, pool_scale, *, seq, tm=512):
    m, d = x.shape
    n_groups, group, _ = w_grp.shape
    pool_width = n_groups * group
    n_pc = b_pc.shape[1]
    conv_width = (n_pc - pool_width) // 2
    assert max(POOL_WINDOWS) <= HALO and seq % tm == 0 and tm % HALO == 0
    halo_blocks = tm // HALO

    return pl.pallas_call(
        functools.partial(_mix_in_kernel, tiles_per_seq=seq // tm),
        grid=(m // tm,),
        in_specs=[
            pl.BlockSpec((tm, d), lambda i: (i, 0)),
            pl.BlockSpec((HALO, d), lambda i: (jnp.maximum(i * halo_blocks - 1, 0), 0)),
            _resident((1, d)),
            _resident((d, n_pc)),
            _resident(b_pc.shape),
            _resident(w_grp.shape),
            _resident((1, pool_width)),
        ],
        out_specs=[
            pl.BlockSpec((tm, pool_width), lambda i: (i, 0)),
            pl.BlockSpec((tm, conv_width), lambda i: (i, 0)),
        ],
        out_shape=[
            jax.ShapeDtypeStruct((m, pool_width), _BF16),
            jax.ShapeDtypeStruct((m, conv_width), _F32),
        ],
        scratch_shapes=[
            pltpu.VMEM((HALO + tm, d), _BF16),
            pltpu.VMEM((HALO + tm, pool_width), _F32),
            pltpu.VMEM((HALO + tm, pool_width), _F32),
            pltpu.VMEM((HALO + tm, pool_width), _F32),
        ],
        compiler_params=pltpu.CompilerParams(
            dimension_semantics=("parallel",), vmem_limit_bytes=VMEM_LIMIT_BYTES),
        name="mix_in",
    )(x, x, g, w_in, b_pc, w_grp, pool_scale)


def _mix_out_kernel(x_ref, g_ref, wgate_refs, bgate_ref, mixed_ref, glu_ref, ghalo_ref, dw_ref,
                    cb_ref, lng_ref, lnb_ref, pp_ref, cp_ref, wo_ref, o_ref,
                    m_ref, gl_ref, conv_ref, c_ref, ga_ref, gb_ref, h_ref,
                    *, tiles_per_seq, col_chunk, ln_rows):
    i = pl.program_id(0)
    tm, d = x_ref.shape
    conv_width = glu_ref.shape[1]
    n_lane_blocks = conv_width // LANES
    n_taps = dw_ref.shape[0]
    first_in_seq = (i % tiles_per_seq) == 0

    for cb in range(n_lane_blocks):
        cs = slice(cb * LANES, (cb + 1) * LANES)
        gl_ref[cb, 0:HALO, :] = jnp.where(first_in_seq, 0.0, ghalo_ref[:, cs])
        gl_ref[cb, HALO:, :] = glu_ref[:, cs]

    rows_per_group = min(tm, CONV_ACCUMULATORS * SUBLANES)

    def conv_lane_block(cb):
        bias = cb_ref[pl.ds(cb, SUBLANES, stride=0), :]
        block_rows = SUBLANES * ROW_STRIDE
        for r0 in range(0, tm, rows_per_group):
            blocks = range(rows_per_group // block_rows)
            accs = [[bias] * ROW_STRIDE for _ in blocks]
            taps = {}
            for q in range(n_taps + ROW_STRIDE - 1):
                if q < n_taps:
                    taps[q] = dw_ref[q, pl.ds(cb, SUBLANES, stride=0), :]
                for blk in blocks:
                    lo = r0 + blk * block_rows + HALO - (n_taps - 1) + q
                    v = gl_ref[cb, pl.ds(lo, SUBLANES, stride=ROW_STRIDE), :]
                    for ph in range(ROW_STRIDE):
                        if 0 <= q - ph < n_taps:
                            accs[blk][ph] = accs[blk][ph] + taps[q - ph] * v
            for blk in blocks:
                for ph in range(ROW_STRIDE):
                    rows = pl.ds(r0 + blk * block_rows + ph, SUBLANES, stride=ROW_STRIDE)
                    conv_ref[cb, rows, :] = accs[blk][ph]

    h_ref[...] = _rmsnorm(x_ref[...], g_ref[...]).astype(_BF16)
    gate_block = wgate_refs[0].shape[1]

    def gate_weight(col):
        ref, lo = wgate_refs[col // gate_block], col % gate_block
        return ref[:, lo:lo + col_chunk]

    n_chunks = d // col_chunk
    convs_per_chunk = n_lane_blocks // n_chunks
    for n in range(n_chunks):
        ca = slice(n * col_chunk, (n + 1) * col_chunk)
        cb = slice(d + n * col_chunk, d + (n + 1) * col_chunk)
        for k in range(convs_per_chunk):
            conv_lane_block(n * convs_per_chunk + k)
        h = h_ref[...]
        gate_a = jax.nn.sigmoid(_dot(h, gate_weight(ca.start)) + bgate_ref[:, ca])
        ga_ref[:, ca] = gate_a * _dot(mixed_ref[...], pp_ref[:, ca])
        gb_ref[:, ca] = jax.nn.sigmoid(_dot(h, gate_weight(cb.start)) + bgate_ref[:, cb])

    for r0 in range(0, tm, ln_rows):
        y = [conv_ref[cb, r0:r0 + ln_rows, :] for cb in range(n_lane_blocks)]
        mu = jnp.sum(sum(y), axis=-1, keepdims=True) * (1.0 / conv_width)
        yc = [v - mu for v in y]
        var = jnp.sum(sum(v * v for v in yc), axis=-1, keepdims=True) * (1.0 / conv_width)
        inv = lax.rsqrt(var + EPS)
        for cb in range(n_lane_blocks):
            cs = slice(cb * LANES, (cb + 1) * LANES)
            z = (yc[cb] * inv) * lng_ref[:, cs] + lnb_ref[:, cs]
            c_ref[r0:r0 + ln_rows, cs] = (z * jax.nn.sigmoid(z)).astype(_BF16)

    c = c_ref[...]
    for n in range(n_chunks):
        ca = slice(n * col_chunk, (n + 1) * col_chunk)
        bb = _dot(c, cp_ref[:, ca])
        m_ref[:, ca] = (ga_ref[:, ca] + gb_ref[:, ca] * bb).astype(_BF16)
    o_ref[...] = x_ref[...] + _dot(m_ref[...], wo_ref[...])


def _mix_out(x, g, w_in, gate_col, b_gate, mixed, glu, conv_dw, conv_b, ln_g, ln_b, pool_w_proj,
             conv_w_proj, w_out, *, seq, tm=256):
    m, d = x.shape
    pool_width = mixed.shape[1]
    n_taps, conv_width = conv_dw.shape
    assert n_taps - 1 <= HALO and seq % tm == 0 and tm % (SUBLANES * ROW_STRIDE) == 0
    halo_blocks = tm // HALO
    n_lane_blocks = conv_width // LANES
    col_chunk = 512
    gate_block = math.gcd(gate_col, d)
    n_gate_blocks = 2 * d // gate_block
    assert gate_block % col_chunk == 0 and gate_col + 2 * d == w_in.shape[1]
    assert d % col_chunk == 0 and n_lane_blocks == 2 * (d // col_chunk)

    return pl.pallas_call(
        functools.partial(_mix_out_kernel, tiles_per_seq=seq // tm, col_chunk=col_chunk, ln_rows=32),
        grid=(m // tm,),
        in_specs=[
            pl.BlockSpec((tm, d), lambda i: (i, 0)),
            _resident((1, d)),
            [pl.BlockSpec((d, gate_block), lambda i, k=k: (0, gate_col // gate_block + k),
                          pipeline_mode=pl.Buffered(1)) for k in range(n_gate_blocks)],
            _resident(b_gate.shape),
            pl.BlockSpec((tm, pool_width), lambda i: (i, 0)),
            pl.BlockSpec((tm, conv_width), lambda i: (i, 0)),
            pl.BlockSpec((HALO, conv_width), lambda i: (jnp.maximum(i * halo_blocks - 1, 0), 0)),
            _resident((n_taps, n_lane_blocks, LANES)),
            _resident((n_lane_blocks, LANES)),
            _resident((1, conv_width)),
            _resident((1, conv_width)),
            _resident(pool_w_proj.shape),
            _resident(conv_w_proj.shape),
            _resident(w_out.shape),
        ],
        out_specs=pl.BlockSpec((tm, d), lambda i: (i, 0)),
        out_shape=jax.ShapeDtypeStruct((m, d), _F32),
        scratch_shapes=[
            pltpu.VMEM((tm, d), _BF16),
            pltpu.VMEM((n_lane_blocks, HALO + tm, LANES), _F32),
            pltpu.VMEM((n_lane_blocks, tm, LANES), _F32),
            pltpu.VMEM((tm, conv_width), _BF16),
            pltpu.VMEM((tm, d), _F32),
            pltpu.VMEM((tm, d), _F32),
            pltpu.VMEM((tm, d), _BF16),
        ],
        compiler_params=pltpu.CompilerParams(
            dimension_semantics=("parallel",), vmem_limit_bytes=VMEM_LIMIT_BYTES),
        name="mix_out",
    )(x, g, [w_in] * n_gate_blocks, b_gate, mixed, glu, glu, conv_dw.reshape(n_taps, n_lane_blocks, LANES),
      conv_b.reshape(n_lane_blocks, LANES), ln_g, ln_b, pool_w_proj, conv_w_proj, w_out)


def kernel(x, ffn1_norm, ffn1_w_in, ffn1_w_out, mix_norm, w_in, b_in, pool_w_grp, pool_scale,
           pool_w_proj, conv_dw, conv_b, conv_ln_g, conv_ln_b, conv_w_proj, w_out, ffn2_norm,
           ffn2_w_in, ffn2_w_out, final_norm):
    batch, seq, d = x.shape
    depth = ffn1_norm.shape[0]
    if depth == 0:
        raise ValueError("depth must be positive")
    pool_width = pool_w_proj.shape[1]
    conv_width = conv_w_proj.shape[1]
    n_pc = pool_width + 2 * conv_width

    def row(v):
        return v.reshape(1, -1).astype(_F32)

    xf = x.reshape(batch * seq, d)
    for l in range(depth):
        xf = _ffn(xf, row(ffn1_norm[l]), ffn1_w_in[l], ffn1_w_out[l])
        w_in_bf16 = w_in[l].astype(_BF16)
        mixed, glu = _mix_in(
            xf, row(mix_norm[l]), w_in_bf16, row(b_in[l][:n_pc]),
            pool_w_grp[l].astype(_BF16), row(pool_scale[l]), seq=seq)
        xf = _mix_out(
            xf, row(mix_norm[l]), w_in_bf16, n_pc, row(b_in[l][n_pc:]), mixed, glu,
            conv_dw[l], conv_b[l], row(conv_ln_g[l]), row(conv_ln_b[l]),
            pool_w_proj[l].astype(_BF16), conv_w_proj[l].astype(_BF16), w_out[l].astype(_BF16),
            seq=seq)
        final_g = row(final_norm) if l == depth - 1 else None
        xf = _ffn(xf, row(ffn2_norm[l]), ffn2_w_in[l], ffn2_w_out[l], final_g)
    return xf.reshape(batch, seq, d)
```

```python
import functools
import math

import jax
import jax.numpy as jnp
from jax import lax
from jax.experimental import pallas as pl
from jax.experimental.pallas import tpu as pltpu

EPS = 1e-6
FFN_RESIDUAL_WEIGHT = 0.5
POOL_WINDOWS = (2, 4, 8, 16)

HALO = 32
SUBLANES = 8
LANES = 128
ROW_STRIDE = 4
CONV_ACCUMULATORS = 16
VMEM_LIMIT_BYTES = 58 * 1024 * 1024

_F32 = jnp.float32
_BF16 = jnp.bfloat16


def _rmsnorm(x, g):
    ms = jnp.mean(x * x, axis=-1, keepdims=True)
    return (x * lax.rsqrt(ms + EPS)) * g


def _dot(a, b):
    return jnp.dot(a, b, preferred_element_type=_F32)


def _resident(shape):
    return pl.BlockSpec(shape, lambda *_: (0,) * len(shape), pipeline_mode=pl.Buffered(1))


def _ffn_body(j, n_j, x_ref, g_ref, fg_ref, wg_ref, wu_ref, wo_ref, o_ref, h_ref,
              *, row_chunk, col_chunk, first_rows, act_chunk):
    tm, d = x_ref.shape
    n_row_chunks = tm // row_chunk

    def chunk_update(h, rows, first=False, last=False):
        tf = wg_ref.shape[1]
        acts = []
        for c0 in range(0, tf, act_chunk):
            gate = _dot(h, wg_ref[:, c0:c0 + act_chunk])
            up = _dot(h, wu_ref[:, c0:c0 + act_chunk])
            acts.append(((gate * jax.nn.sigmoid(gate)) * up).astype(_BF16))
        act = acts[0] if len(acts) == 1 else jnp.concatenate(acts, axis=1)
        for n in range(d // col_chunk):
            cs = slice(n * col_chunk, (n + 1) * col_chunk)
            y = _dot(act, wo_ref[:, cs])
            if not first:
                y = o_ref[rows, cs] + y
            if last:
                y = x_ref[rows, cs] + FFN_RESIDUAL_WEIGHT * y
            o_ref[rows, cs] = y

    @pl.when(j == 0)
    def _():
        for r0 in range(0, tm, first_rows):
            rows = slice(r0, r0 + first_rows)
            h = _rmsnorm(x_ref[rows, :], g_ref[...]).astype(_BF16)
            h_ref[rows, :] = h
            chunk_update(h, rows, first=True)

    @pl.when(jnp.logical_and(j > 0, j < n_j - 1))
    def _():
        chunk_update(h_ref[...], slice(None))

    @pl.when(j == n_j - 1)
    def _():
        chunk_update(h_ref[...], slice(None), last=True)
        if fg_ref is not None:
            def body(r, carry):
                r0 = pl.multiple_of(r * row_chunk, row_chunk)
                rows = pl.ds(r0, row_chunk)
                o_ref[rows, :] = _rmsnorm(o_ref[rows, :], fg_ref[...])
                return carry

            lax.fori_loop(0, n_row_chunks, body, 0)


def _ffn_head_kernel(x_ref, g_ref, wg32_ref, wu32_ref, wo32_ref, *rest, final_norm, **chunks):
    fg_ref = rest[0] if final_norm else None
    o_ref, wg_ref, wu_ref, wo_ref, h_ref = rest[1:] if final_norm else rest
    wg_ref[...] = wg32_ref[...].astype(_BF16)
    wu_ref[...] = wu32_ref[...].astype(_BF16)
    wo_ref[...] = wo32_ref[...].astype(_BF16)
    _ffn_body(pl.program_id(0), pl.num_programs(0), x_ref, g_ref, fg_ref, wg_ref, wu_ref, wo_ref,
              o_ref, h_ref, **chunks)


def _ffn_tail_kernel(x_ref, y0_ref, g_ref, wg_ref, wu_ref, wo_ref, *rest, final_norm, **chunks):
    fg_ref = rest[0] if final_norm else None
    o_ref, h_ref = rest[1:] if final_norm else rest
    i, j = pl.program_id(0), pl.program_id(1)
    copy_cols = y0_ref.shape[1]

    @pl.when(i == 0)
    def _():
        for k in range(o_ref.shape[1] // copy_cols):
            @pl.when(j == k)
            def _(k=k):
                o_ref[:, k * copy_cols:(k + 1) * copy_cols] = y0_ref[...]

    @pl.when(i > 0)
    def _():
        _ffn_body(j, pl.num_programs(1), x_ref, g_ref, fg_ref, wg_ref, wu_ref, wo_ref, o_ref, h_ref,
                  **chunks)


def _ffn(x, g, w_in, w_out, final_g=None, *, tm=1024, tf=512, tf_head=256):
    m, d = x.shape
    f = w_out.shape[0]
    final_norm = final_g is not None
    chunks = dict(row_chunk=128, col_chunk=512, first_rows=256, act_chunk=256)
    extra_specs = [pl.BlockSpec((1, d), lambda *_: (0, 0))] if final_norm else []
    extra_args = [final_g] if final_norm else []
    suffix = "_final" if final_norm else ""

    n_head = f // tf_head
    assert f % tf == 0 and f % tf_head == 0 and f // tf >= 2 and n_head >= 2
    y, wg, wu, wo = pl.pallas_call(
        functools.partial(_ffn_head_kernel, final_norm=final_norm, **chunks),
        grid=(n_head,),
        in_specs=[
            pl.BlockSpec((tm, d), lambda j: (0, 0), pipeline_mode=pl.Buffered(1)),
            pl.BlockSpec((1, d), lambda j: (0, 0)),
            pl.BlockSpec((d, tf_head), lambda j: (0, j)),
            pl.BlockSpec((d, tf_head), lambda j: (0, j + n_head)),
            pl.BlockSpec((tf_head, d), lambda j: (j, 0)),
        ] + extra_specs,
        out_specs=[
            pl.BlockSpec((tm, d), lambda j: (0, 0)),
            pl.BlockSpec((d, tf_head), lambda j: (0, j)),
            pl.BlockSpec((d, tf_head), lambda j: (0, j)),
            pl.BlockSpec((tf_head, d), lambda j: (j, 0)),
        ],
        out_shape=[
            jax.ShapeDtypeStruct((tm, d), _F32),
            jax.ShapeDtypeStruct((d, f), _BF16),
            jax.ShapeDtypeStruct((d, f), _BF16),
            jax.ShapeDtypeStruct((f, d), _BF16),
        ],
        scratch_shapes=[pltpu.VMEM((tm, d), _BF16)],
        compiler_params=pltpu.CompilerParams(
            dimension_semantics=("arbitrary",), vmem_limit_bytes=VMEM_LIMIT_BYTES),
        name="ffn_head" + suffix,
    )(x, g, w_in, w_in, w_out, *extra_args)

    n_f = f // tf
    copy_cols = 256
    n_copy = d // copy_cols
    assert n_copy <= n_f

    def chunk(i, j):
        return jnp.where(i == 0, 0, j)

    return pl.pallas_call(
        functools.partial(_ffn_tail_kernel, final_norm=final_norm, **chunks),
        grid=(m // tm, n_f),
        in_specs=[
            pl.BlockSpec((tm, d), lambda i, j: (i, 0)),
            pl.BlockSpec((tm, copy_cols),
                         lambda i, j: (0, jnp.where(i == 0, jnp.minimum(j, n_copy - 1), n_copy - 1))),
            pl.BlockSpec((1, d), lambda i, j: (0, 0)),
            pl.BlockSpec((d, tf), lambda i, j: (0, chunk(i, j))),
            pl.BlockSpec((d, tf), lambda i, j: (0, chunk(i, j))),
            pl.BlockSpec((tf, d), lambda i, j: (chunk(i, j), 0)),
        ] + extra_specs,
        out_specs=pl.BlockSpec((tm, d), lambda i, j: (i, 0)),
        out_shape=jax.ShapeDtypeStruct((m, d), _F32),
        scratch_shapes=[pltpu.VMEM((tm, d), _BF16)],
        compiler_params=pltpu.CompilerParams(
            dimension_semantics=("parallel", "arbitrary"), vmem_limit_bytes=VMEM_LIMIT_BYTES),
        name="ffn" + suffix,
    )(x, y, g, wg, wu, wo, *extra_args)


def _mix_in_kernel(x_ref, halo_ref, g_ref, w_ref, b_ref, wgrp_ref, pscale_ref, mixed_ref, glu_ref,
                   h_ref, u_ref, s_ref, t_ref, *, tiles_per_seq, norm_rows):
    i = pl.program_id(0)
    tm = x_ref.shape[0]
    pool_width = mixed_ref.shape[1]
    conv_width = glu_ref.shape[1]
    group = pool_width // len(POOL_WINDOWS)
    tile_in_seq = i % tiles_per_seq

    h_ref[0:HALO, :] = _rmsnorm(halo_ref[...], g_ref[...]).astype(_BF16)
    for r0 in range(0, tm, norm_rows):
        rows = slice(HALO + r0, HALO + r0 + norm_rows)
        h_ref[rows, :] = _rmsnorm(x_ref[r0:r0 + norm_rows, :], g_ref[...]).astype(_BF16)
        if r0 == 0:
            rows = slice(0, HALO + norm_rows)
        u = _dot(h_ref[rows, :], w_ref[:, 0:pool_width]) + b_ref[:, 0:pool_width]
        if r0 == 0:
            u_ref[0:HALO, :] = jnp.where(tile_in_seq == 0, 0.0, u[0:HALO])
            u = u[HALO:]
        u_ref[HALO + r0:HALO + r0 + norm_rows, :] = u

    pos = tile_in_seq * tm + lax.broadcasted_iota(jnp.int32, (tm, 1), 0)

    def pool_group(gi, window):
        cs = slice(gi * group, (gi + 1) * group)
        src, shift, lo, level = u_ref, 1, 8, 0
        while shift < window:
            dst = (s_ref, t_ref)[level % 2]
            dst[lo:, cs] = src[lo:, cs] + src[lo - shift:HALO + tm - shift, cs]
            src, shift, lo, level = dst, shift * 2, lo + 8, level + 1
        inv_count = 1.0 / jnp.minimum(pos + 1, window).astype(_F32)
        pooled = src[HALO:, cs] * inv_count - u_ref[HALO:, cs]
        mixed = _dot(pooled.astype(_BF16), wgrp_ref[gi])
        mixed_ref[:, cs] = (mixed * pscale_ref[:, cs]).astype(_BF16)

    h = h_ref[HALO:, :]
    glu_chunk = conv_width // len(POOL_WINDOWS)
    for gi, window in enumerate(POOL_WINDOWS):
        c0 = gi * glu_chunk
        ca = slice(pool_width + c0, pool_width + c0 + glu_chunk)
        cg = slice(pool_width + conv_width + c0, pool_width + conv_width + c0 + glu_chunk)
        za = _dot(h, w_ref[:, ca]) + b_ref[:, ca]
        zg = _dot(h, w_ref[:, cg]) + b_ref[:, cg]
        glu_ref[:, c0:c0 + glu_chunk] = za * jax.nn.sigmoid(zg)
        pool_group(gi, window)


def _mix_in(x, g, w_in, b_pc, w_grp, pool_scale, *, seq, tm=512):
    m, d = x.shape
    n_groups, group, _ = w_grp.shape
    pool_width = n_groups * group
    n_pc = b_pc.shape[1]
    conv_width = (n_pc - pool_width) // 2
    assert max(POOL_WINDOWS) <= HALO and seq % tm == 0 and tm % HALO == 0
    halo_blocks = tm // HALO

    return pl.pallas_call(
        functools.partial(_mix_in_kernel, tiles_per_seq=seq // tm, norm_rows=256),
        grid=(m // tm,),
        in_specs=[
            pl.BlockSpec((tm, d), lambda i: (i, 0)),
            pl.BlockSpec((HALO, d), lambda i: (jnp.maximum(i * halo_blocks - 1, 0), 0)),
            _resident((1, d)),
            _resident((d, n_pc)),
            _resident(b_pc.shape),
            _resident(w_grp.shape),
            _resident((1, pool_width)),
        ],
        out_specs=[
            pl.BlockSpec((tm, pool_width), lambda i: (i, 0)),
            pl.BlockSpec((tm, conv_width), lambda i: (i, 0)),
        ],
        out_shape=[
            jax.ShapeDtypeStruct((m, pool_width), _BF16),
            jax.ShapeDtypeStruct((m, conv_width), _F32),
        ],
        scratch_shapes=[
            pltpu.VMEM((HALO + tm, d), _BF16),
            pltpu.VMEM((HALO + tm, pool_width), _F32),
            pltpu.VMEM((HALO + tm, pool_width), _F32),
            pltpu.VMEM((HALO + tm, pool_width), _F32),
        ],
        compiler_params=pltpu.CompilerParams(
            dimension_semantics=("parallel",), vmem_limit_bytes=VMEM_LIMIT_BYTES),
        name="mix_in",
    )(x, x, g, w_in, b_pc, w_grp, pool_scale)


def _mix_out_kernel(x_ref, g_ref, wgate_refs, bgate_ref, mixed_ref, glu_ref, ghalo_ref, dw_ref,
                    cb_ref, lng_ref, lnb_ref, pp_ref, cp_ref, wo_ref, o_ref,
                    m_ref, gl_ref, conv_ref, c_ref, ga_ref, gb_ref, h_ref,
                    *, tiles_per_seq, col_chunk, ln_rows):
    i = pl.program_id(0)
    tm, d = x_ref.shape
    conv_width = glu_ref.shape[1]
    n_lane_blocks = conv_width // LANES
    n_taps = dw_ref.shape[0]
    first_in_seq = (i % tiles_per_seq) == 0

    for cb in range(n_lane_blocks):
        cs = slice(cb * LANES, (cb + 1) * LANES)
        gl_ref[cb, 0:HALO, :] = jnp.where(first_in_seq, 0.0, ghalo_ref[:, cs])
        gl_ref[cb, HALO:, :] = glu_ref[:, cs]

    rows_per_group = min(tm, CONV_ACCUMULATORS * SUBLANES)

    def conv_lane_block(cb):
        bias = cb_ref[pl.ds(cb, SUBLANES, stride=0), :]
        block_rows = SUBLANES * ROW_STRIDE
        for r0 in range(0, tm, rows_per_group):
            blocks = range(rows_per_group // block_rows)
            accs = [[bias] * ROW_STRIDE for _ in blocks]
            taps = {}
            for q in range(n_taps + ROW_STRIDE - 1):
                if q < n_taps:
                    taps[q] = dw_ref[q, pl.ds(cb, SUBLANES, stride=0), :]
                for blk in blocks:
                    lo = r0 + blk * block_rows + HALO - (n_taps - 1) + q
                    v = gl_ref[cb, pl.ds(lo, SUBLANES, stride=ROW_STRIDE), :]
                    for ph in range(ROW_STRIDE):
                        if 0 <= q - ph < n_taps:
                            accs[blk][ph] = accs[blk][ph] + taps[q - ph] * v
            for blk in blocks:
                for ph in range(ROW_STRIDE):
                    rows = pl.ds(r0 + blk * block_rows + ph, SUBLANES, stride=ROW_STRIDE)
                    conv_ref[cb, rows, :] = accs[blk][ph]

    h_ref[...] = _rmsnorm(x_ref[...], g_ref[...]).astype(_BF16)
    gate_block = wgate_refs[0].shape[1]

    def gate_weight(col):
        ref, lo = wgate_refs[col // gate_block], col % gate_block
        return ref[:, lo:lo + col_chunk]

    n_chunks = d // col_chunk
    convs_per_chunk = n_lane_blocks // n_chunks
    for n in range(n_chunks):
        ca = slice(n * col_chunk, (n + 1) * col_chunk)
        cb = slice(d + n * col_chunk, d + (n + 1) * col_chunk)
        for k in range(convs_per_chunk):
            conv_lane_block(n * convs_per_chunk + k)
        h = h_ref[...]
        gate_a = jax.nn.sigmoid(_dot(h, gate_weight(ca.start)) + bgate_ref[:, ca])
        ga_ref[:, ca] = gate_a * _dot(mixed_ref[...], pp_ref[:, ca])
        gb_ref[:, ca] = jax.nn.sigmoid(_dot(h, gate_weight(cb.start)) + bgate_ref[:, cb])

    for r0 in range(0, tm, ln_rows):
        y = [conv_ref[cb, r0:r0 + ln_rows, :] for cb in range(n_lane_blocks)]
        mu = jnp.sum(sum(y), axis=-1, keepdims=True) * (1.0 / conv_width)
        yc = [v - mu for v in y]
        var = jnp.sum(sum(v * v for v in yc), axis=-1, keepdims=True) * (1.0 / conv_width)
        inv = lax.rsqrt(var + EPS)
        for cb in range(n_lane_blocks):
            cs = slice(cb * LANES, (cb + 1) * LANES)
            z = (yc[cb] * inv) * lng_ref[:, cs] + lnb_ref[:, cs]
            c_ref[r0:r0 + ln_rows, cs] = (z * jax.nn.sigmoid(z)).astype(_BF16)

    c = c_ref[...]
    for n in range(n_chunks):
        ca = slice(n * col_chunk, (n + 1) * col_chunk)
        bb = _dot(c, cp_ref[:, ca])
        m_ref[:, ca] = (ga_ref[:, ca] + gb_ref[:, ca] * bb).astype(_BF16)
    o_ref[...] = x_ref[...] + _dot(m_ref[...], wo_ref[...])


def _mix_out(x, g, w_in, gate_col, b_gate, mixed, glu, conv_dw, conv_b, ln_g, ln_b, pool_w_proj,
             conv_w_proj, w_out, *, seq, tm=256):
    m, d = x.shape
    pool_width = mixed.shape[1]
    n_taps, conv_width = conv_dw.shape
    assert n_taps - 1 <= HALO and seq % tm == 0 and tm % (SUBLANES * ROW_STRIDE) == 0
    halo_blocks = tm // HALO
    n_lane_blocks = conv_width // LANES
    col_chunk = 512
    gate_block = math.gcd(gate_col, d)
    n_gate_blocks = 2 * d // gate_block
    assert gate_block % col_chunk == 0 and gate_col + 2 * d == w_in.shape[1]
    assert d % col_chunk == 0 and n_lane_blocks == 2 * (d // col_chunk)

    return pl.pallas_call(
        functools.partial(_mix_out_kernel, tiles_per_seq=seq // tm, col_chunk=col_chunk, ln_rows=32),
        grid=(m // tm,),
        in_specs=[
            pl.BlockSpec((tm, d), lambda i: (i, 0)),
            _resident((1, d)),
            [pl.BlockSpec((d, gate_block), lambda i, k=k: (0, gate_col // gate_block + k),
                          pipeline_mode=pl.Buffered(1)) for k in range(n_gate_blocks)],
            _resident(b_gate.shape),
            pl.BlockSpec((tm, pool_width), lambda i: (i, 0)),
            pl.BlockSpec((tm, conv_width), lambda i: (i, 0)),
            pl.BlockSpec((HALO, conv_width), lambda i: (jnp.maximum(i * halo_blocks - 1, 0), 0)),
            _resident((n_taps, n_lane_blocks, LANES)),
            _resident((n_lane_blocks, LANES)),
            _resident((1, conv_width)),
            _resident((1, conv_width)),
            _resident(pool_w_proj.shape),
            _resident(conv_w_proj.shape),
            _resident(w_out.shape),
        ],
        out_specs=pl.BlockSpec((tm, d), lambda i: (i, 0)),
        out_shape=jax.ShapeDtypeStruct((m, d), _F32),
        scratch_shapes=[
            pltpu.VMEM((tm, d), _BF16),
            pltpu.VMEM((n_lane_blocks, HALO + tm, LANES), _F32),
            pltpu.VMEM((n_lane_blocks, tm, LANES), _F32),
            pltpu.VMEM((tm, conv_width), _BF16),
            pltpu.VMEM((tm, d), _F32),
            pltpu.VMEM((tm, d), _F32),
            pltpu.VMEM((tm, d), _BF16),
        ],
        compiler_params=pltpu.CompilerParams(
            dimension_semantics=("parallel",), vmem_limit_bytes=VMEM_LIMIT_BYTES),
        name="mix_out",
    )(x, g, [w_in] * n_gate_blocks, b_gate, mixed, glu, glu, conv_dw.reshape(n_taps, n_lane_blocks, LANES),
      conv_b.reshape(n_lane_blocks, LANES), ln_g, ln_b, pool_w_proj, conv_w_proj, w_out)


def kernel(x, ffn1_norm, ffn1_w_in, ffn1_w_out, mix_norm, w_in, b_in, pool_w_grp, pool_scale,
           pool_w_proj, conv_dw, conv_b, conv_ln_g, conv_ln_b, conv_w_proj, w_out, ffn2_norm,
           ffn2_w_in, ffn2_w_out, final_norm):
    batch, seq, d = x.shape
    depth = ffn1_norm.shape[0]
    if depth == 0:
        raise ValueError("depth must be positive")
    pool_width = pool_w_proj.shape[1]
    conv_width = conv_w_proj.shape[1]
    n_pc = pool_width + 2 * conv_width

    def row(v):
        return v.reshape(1, -1).astype(_F32)

    xf = x.reshape(batch * seq, d)
    for l in range(depth):
        xf = _ffn(xf, row(ffn1_norm[l]), ffn1_w_in[l], ffn1_w_out[l])
        w_in_bf16 = w_in[l].astype(_BF16)
        mixed, glu = _mix_in(
            xf, row(mix_norm[l]), w_in_bf16, row(b_in[l][:n_pc]),
            pool_w_grp[l].astype(_BF16), row(pool_scale[l]), seq=seq)
        xf = _mix_out(
            xf, row(mix_norm[l]), w_in_bf16, n_pc, row(b_in[l][n_pc:]), mixed, glu,
            conv_dw[l], conv_b[l], row(conv_ln_g[l]), row(conv_ln_b[l]),
            pool_w_proj[l].astype(_BF16), conv_w_proj[l].astype(_BF16), w_out[l].astype(_BF16),
            seq=seq)
        final_g = row(final_norm) if l == depth - 1 else None
        xf = _ffn(xf, row(ffn2_norm[l]), ffn2_w_in[l], ffn2_w_out[l], final_g)
    return xf.reshape(batch, seq, d)
```

```python
import functools
import math

import jax
import jax.numpy as jnp
from jax import lax
from jax.experimental import pallas as pl
from jax.experimental.pallas import tpu as pltpu

EPS = 1e-6
FFN_RESIDUAL_WEIGHT = 0.5
POOL_WINDOWS = (2, 4, 8, 16)

HALO = 32
SUBLANES = 8
LANES = 128
ROW_STRIDE = 4
CONV_ACCUMULATORS = 16
VMEM_LIMIT_BYTES = 58 * 1024 * 1024

_F32 = jnp.float32
_BF16 = jnp.bfloat16


def _rmsnorm(x, g):
    ms = jnp.mean(x * x, axis=-1, keepdims=True)
    return (x * lax.rsqrt(ms + EPS)) * g


def _dot(a, b):
    return jnp.dot(a, b, preferred_element_type=_F32)


def _resident(shape):
    return pl.BlockSpec(shape, lambda *_: (0,) * len(shape), pipeline_mode=pl.Buffered(1))


def _ffn_body(j, n_j, x_ref, g_ref, fg_ref, wg_ref, wu_ref, wo_ref, o_ref, h_ref,
              *, row_chunk, col_chunk, first_rows):
    tm, d = x_ref.shape
    n_row_chunks = tm // row_chunk

    def chunk_update(h, rows, first=False, last=False):
        acts = []
        for k in range(wg_ref.shape[0]):
            gate = _dot(h, wg_ref[k])
            up = _dot(h, wu_ref[k])
            acts.append(((gate * jax.nn.sigmoid(gate)) * up).astype(_BF16))
        act = acts[0] if len(acts) == 1 else jnp.concatenate(acts, axis=1)
        for n in range(d // col_chunk):
            cs = slice(n * col_chunk, (n + 1) * col_chunk)
            y = _dot(act, wo_ref[:, cs])
            if not first:
                y = o_ref[rows, cs] + y
            if last:
                y = x_ref[rows, cs] + FFN_RESIDUAL_WEIGHT * y
            o_ref[rows, cs] = y

    @pl.when(j == 0)
    def _():
        for r0 in range(0, tm, first_rows):
            rows = slice(r0, r0 + first_rows)
            h = _rmsnorm(x_ref[rows, :], g_ref[...]).astype(_BF16)
            h_ref[rows, :] = h
            chunk_update(h, rows, first=True)

    @pl.when(jnp.logical_and(j > 0, j < n_j - 1))
    def _():
        chunk_update(h_ref[...], slice(None))

    @pl.when(j == n_j - 1)
    def _():
        chunk_update(h_ref[...], slice(None), last=True)
        if fg_ref is not None:
            def body(r, carry):
                r0 = pl.multiple_of(r * row_chunk, row_chunk)
                rows = pl.ds(r0, row_chunk)
                o_ref[rows, :] = _rmsnorm(o_ref[rows, :], fg_ref[...])
                return carry

            lax.fori_loop(0, n_row_chunks, body, 0)


def _ffn_head_kernel(x_ref, g_ref, wg32_ref, wu32_ref, wo32_ref, *rest, final_norm, **chunks):
    fg_ref = rest[0] if final_norm else None
    o_ref, wg_ref, wu_ref, wo_ref, h_ref = rest[1:] if final_norm else rest
    wg_ref[0] = wg32_ref[...].astype(_BF16)
    wu_ref[0] = wu32_ref[...].astype(_BF16)
    wo_ref[...] = wo32_ref[...].astype(_BF16)
    _ffn_body(pl.program_id(0), pl.num_programs(0), x_ref, g_ref, fg_ref, wg_ref, wu_ref, wo_ref,
              o_ref, h_ref, **chunks)


def _ffn_tail_kernel(x_ref, y0_ref, g_ref, wg_ref, wu_ref, wo_ref, *rest, final_norm, **chunks):
    fg_ref = rest[0] if final_norm else None
    o_ref, h_ref = rest[1:] if final_norm else rest
    i, j = pl.program_id(0), pl.program_id(1)
    copy_cols = y0_ref.shape[1]

    @pl.when(i == 0)
    def _():
        for k in range(o_ref.shape[1] // copy_cols):
            @pl.when(j == k)
            def _(k=k):
                o_ref[:, k * copy_cols:(k + 1) * copy_cols] = y0_ref[...]

    @pl.when(i > 0)
    def _():
        _ffn_body(j, pl.num_programs(1), x_ref, g_ref, fg_ref, wg_ref, wu_ref, wo_ref, o_ref, h_ref,
                  **chunks)


def _ffn(x, g, w_in, w_out, final_g=None, *, tm=1024, tf=512, tf_head=256):
    m, d = x.shape
    f = w_out.shape[0]
    final_norm = final_g is not None
    chunks = dict(row_chunk=128, col_chunk=512, first_rows=256)
    extra_specs = [pl.BlockSpec((1, d), lambda *_: (0, 0))] if final_norm else []
    extra_args = [final_g] if final_norm else []
    suffix = "_final" if final_norm else ""

    n_head = f // tf_head
    assert f % tf == 0 and tf % tf_head == 0 and f // tf >= 2 and n_head >= 2
    y, wg, wu, wo = pl.pallas_call(
        functools.partial(_ffn_head_kernel, final_norm=final_norm, **chunks),
        grid=(n_head,),
        in_specs=[
            pl.BlockSpec((tm, d), lambda j: (0, 0), pipeline_mode=pl.Buffered(1)),
            pl.BlockSpec((1, d), lambda j: (0, 0)),
            pl.BlockSpec((d, tf_head), lambda j: (0, j)),
            pl.BlockSpec((d, tf_head), lambda j: (0, j + n_head)),
            pl.BlockSpec((tf_head, d), lambda j: (j, 0)),
        ] + extra_specs,
        out_specs=[
            pl.BlockSpec((tm, d), lambda j: (0, 0)),
            pl.BlockSpec((1, d, tf_head), lambda j: (j, 0, 0)),
            pl.BlockSpec((1, d, tf_head), lambda j: (j, 0, 0)),
            pl.BlockSpec((tf_head, d), lambda j: (j, 0)),
        ],
        out_shape=[
            jax.ShapeDtypeStruct((tm, d), _F32),
            jax.ShapeDtypeStruct((n_head, d, tf_head), _BF16),
            jax.ShapeDtypeStruct((n_head, d, tf_head), _BF16),
            jax.ShapeDtypeStruct((f, d), _BF16),
        ],
        scratch_shapes=[pltpu.VMEM((tm, d), _BF16)],
        compiler_params=pltpu.CompilerParams(
            dimension_semantics=("arbitrary",), vmem_limit_bytes=VMEM_LIMIT_BYTES),
        name="ffn_head" + suffix,
    )(x, g, w_in, w_in, w_out, *extra_args)

    n_f = f // tf
    copy_cols = 256
    n_copy = d // copy_cols
    assert n_copy <= n_f

    def chunk(i, j):
        return jnp.where(i == 0, 0, j)

    return pl.pallas_call(
        functools.partial(_ffn_tail_kernel, final_norm=final_norm, **chunks),
        grid=(m // tm, n_f),
        in_specs=[
            pl.BlockSpec((tm, d), lambda i, j: (i, 0)),
            pl.BlockSpec((tm, copy_cols),
                         lambda i, j: (0, jnp.where(i == 0, jnp.minimum(j, n_copy - 1), n_copy - 1))),
            pl.BlockSpec((1, d), lambda i, j: (0, 0)),
            pl.BlockSpec((tf // tf_head, d, tf_head), lambda i, j: (chunk(i, j), 0, 0)),
            pl.BlockSpec((tf // tf_head, d, tf_head), lambda i, j: (chunk(i, j), 0, 0)),
            pl.BlockSpec((tf, d), lambda i, j: (chunk(i, j), 0)),
        ] + extra_specs,
        out_specs=pl.BlockSpec((tm, d), lambda i, j: (i, 0)),
        out_shape=jax.ShapeDtypeStruct((m, d), _F32),
        scratch_shapes=[pltpu.VMEM((tm, d), _BF16)],
        compiler_params=pltpu.CompilerParams(
            dimension_semantics=("parallel", "arbitrary"), vmem_limit_bytes=VMEM_LIMIT_BYTES),
        name="ffn" + suffix,
    )(x, y, g, wg, wu, wo, *extra_args)


def _mix_in_kernel(x_ref, halo_ref, g_ref, w_ref, b_ref, wgrp_ref, pscale_ref, mixed_ref, glu_ref,
                   h_ref, u_ref, s_ref, t_ref, *, tiles_per_seq, norm_rows):
    i = pl.program_id(0)
    tm = x_ref.shape[0]
    pool_width = mixed_ref.shape[1]
    conv_width = glu_ref.shape[1]
    group = pool_width // len(POOL_WINDOWS)
    tile_in_seq = i % tiles_per_seq

    h_ref[0:HALO, :] = _rmsnorm(halo_ref[...], g_ref[...]).astype(_BF16)
    for r0 in range(0, tm, norm_rows):
        rows = slice(HALO + r0, HALO + r0 + norm_rows)
        h_ref[rows, :] = _rmsnorm(x_ref[r0:r0 + norm_rows, :], g_ref[...]).astype(_BF16)
        if r0 == 0:
            rows = slice(0, HALO + norm_rows)
        u = _dot(h_ref[rows, :], w_ref[:, 0:pool_width]) + b_ref[:, 0:pool_width]
        if r0 == 0:
            u_ref[0:HALO, :] = jnp.where(tile_in_seq == 0, 0.0, u[0:HALO])
            u = u[HALO:]
        u_ref[HALO + r0:HALO + r0 + norm_rows, :] = u

    pos = tile_in_seq * tm + lax.broadcasted_iota(jnp.int32, (tm, 1), 0)

    def pool_group(gi, window):
        cs = slice(gi * group, (gi + 1) * group)
        src, shift, lo, level = u_ref, 1, 8, 0
        while shift < window:
            dst = (s_ref, t_ref)[level % 2]
            dst[lo:, cs] = src[lo:, cs] + src[lo - shift:HALO + tm - shift, cs]
            src, shift, lo, level = dst, shift * 2, lo + 8, level + 1
        inv_count = 1.0 / jnp.minimum(pos + 1, window).astype(_F32)
        pooled = src[HALO:, cs] * inv_count - u_ref[HALO:, cs]
        mixed = _dot(pooled.astype(_BF16), wgrp_ref[gi])
        mixed_ref[:, cs] = (mixed * pscale_ref[:, cs]).astype(_BF16)

    h = h_ref[HALO:, :]
    glu_chunk = conv_width // len(POOL_WINDOWS)
    for gi, window in enumerate(POOL_WINDOWS):
        c0 = gi * glu_chunk
        ca = slice(pool_width + c0, pool_width + c0 + glu_chunk)
        cg = slice(pool_width + conv_width + c0, pool_width + conv_width + c0 + glu_chunk)
        za = _dot(h, w_ref[:, ca]) + b_ref[:, ca]
        zg = _dot(h, w_ref[:, cg]) + b_ref[:, cg]
        glu_ref[:, c0:c0 + glu_chunk] = za * jax.nn.sigmoid(zg)
        pool_group(gi, window)


def _mix_in(x, g, w_in, b_pc, w_grp, pool_scale, *, seq, tm=512):
    m, d = x.shape
    n_groups, group, _ = w_grp.shape
    pool_width = n_groups * group
    n_pc = b_pc.shape[1]
    conv_width = (n_pc - pool_width) // 2
    assert max(POOL_WINDOWS) <= HALO and seq % tm == 0 and tm % HALO == 0
    halo_blocks = tm // HALO

    return pl.pallas_call(
        functools.partial(_mix_in_kernel, tiles_per_seq=seq // tm, norm_rows=256),
        grid=(m // tm,),
        in_specs=[
            pl.BlockSpec((tm, d), lambda i: (i, 0)),
            pl.BlockSpec((HALO, d), lambda i: (jnp.maximum(i * halo_blocks - 1, 0), 0)),
            _resident((1, d)),
            _resident((d, n_pc)),
            _resident(b_pc.shape),
            _resident(w_grp.shape),
            _resident((1, pool_width)),
        ],
        out_specs=[
            pl.BlockSpec((tm, pool_width), lambda i: (i, 0)),
            pl.BlockSpec((tm, conv_width), lambda i: (i, 0)),
        ],
        out_shape=[
            jax.ShapeDtypeStruct((m, pool_width), _BF16),
            jax.ShapeDtypeStruct((m, conv_width), _F32),
        ],
        scratch_shapes=[
            pltpu.VMEM((HALO + tm, d), _BF16),
            pltpu.VMEM((HALO + tm, pool_width), _F32),
            pltpu.VMEM((HALO + tm, pool_width), _F32),
            pltpu.VMEM((HALO + tm, pool_width), _F32),
        ],
        compiler_params=pltpu.CompilerParams(
            dimension_semantics=("parallel",), vmem_limit_bytes=VMEM_LIMIT_BYTES),
        name="mix_in",
    )(x, x, g, w_in, b_pc, w_grp, pool_scale)


def _mix_out_kernel(x_ref, g_ref, wgate_refs, bgate_ref, mixed_ref, glu_ref, ghalo_ref, dw_ref,
                    cb_ref, lng_ref, lnb_ref, pp_ref, cp_ref, wo_ref, o_ref,
                    m_ref, gl_ref, conv_ref, c_ref, ga_ref, gb_ref, h_ref,
                    *, tiles_per_seq, col_chunk, ln_rows):
    i = pl.program_id(0)
    tm, d = x_ref.shape
    conv_width = glu_ref.shape[1]
    n_lane_blocks = conv_width // LANES
    n_taps = dw_ref.shape[0]
    first_in_seq = (i % tiles_per_seq) == 0

    for cb in range(n_lane_blocks):
        cs = slice(cb * LANES, (cb + 1) * LANES)
        gl_ref[cb, 0:HALO, :] = jnp.where(first_in_seq, 0.0, ghalo_ref[:, cs])
        gl_ref[cb, HALO:, :] = glu_ref[:, cs]

    rows_per_group = min(tm, CONV_ACCUMULATORS * SUBLANES)

    def conv_lane_block(cb):
        bias = cb_ref[pl.ds(cb, SUBLANES, stride=0), :]
        block_rows = SUBLANES * ROW_STRIDE
        for r0 in range(0, tm, rows_per_group):
            blocks = range(rows_per_group // block_rows)
            accs = [[bias] * ROW_STRIDE for _ in blocks]
            taps = {}
            for q in range(n_taps + ROW_STRIDE - 1):
                if q < n_taps:
                    taps[q] = dw_ref[q, pl.ds(cb, SUBLANES, stride=0), :]
                for blk in blocks:
                    lo = r0 + blk * block_rows + HALO - (n_taps - 1) + q
                    v = gl_ref[cb, pl.ds(lo, SUBLANES, stride=ROW_STRIDE), :]
                    for ph in range(ROW_STRIDE):
                        if 0 <= q - ph < n_taps:
                            accs[blk][ph] = accs[blk][ph] + taps[q - ph] * v
            for blk in blocks:
                for ph in range(ROW_STRIDE):
                    rows = pl.ds(r0 + blk * block_rows + ph, SUBLANES, stride=ROW_STRIDE)
                    conv_ref[cb, rows, :] = accs[blk][ph]

    h_ref[...] = _rmsnorm(x_ref[...], g_ref[...]).astype(_BF16)
    gate_block = wgate_refs[0].shape[1]

    def gate_weight(col):
        ref, lo = wgate_refs[col // gate_block], col % gate_block
        return ref[:, lo:lo + col_chunk]

    n_chunks = d // col_chunk
    convs_per_chunk = n_lane_blocks // n_chunks
    for n in range(n_chunks):
        ca = slice(n * col_chunk, (n + 1) * col_chunk)
        cb = slice(d + n * col_chunk, d + (n + 1) * col_chunk)
        for k in range(convs_per_chunk):
            conv_lane_block(n * convs_per_chunk + k)
        h = h_ref[...]
        gate_a = jax.nn.sigmoid(_dot(h, gate_weight(ca.start)) + bgate_ref[:, ca])
        ga_ref[:, ca] = gate_a * _dot(mixed_ref[...], pp_ref[:, ca])
        gb_ref[:, ca] = jax.nn.sigmoid(_dot(h, gate_weight(cb.start)) + bgate_ref[:, cb])

    for r0 in range(0, tm, ln_rows):
        y = [conv_ref[cb, r0:r0 + ln_rows, :] for cb in range(n_lane_blocks)]
        mu = jnp.sum(sum(y), axis=-1, keepdims=True) * (1.0 / conv_width)
        yc = [v - mu for v in y]
        var = jnp.sum(sum(v * v for v in yc), axis=-1, keepdims=True) * (1.0 / conv_width)
        inv = lax.rsqrt(var + EPS)
        for cb in range(n_lane_blocks):
            cs = slice(cb * LANES, (cb + 1) * LANES)
            z = (yc[cb] * inv) * lng_ref[:, cs] + lnb_ref[:, cs]
            c_ref[r0:r0 + ln_rows, cs] = (z * jax.nn.sigmoid(z)).astype(_BF16)

    c = c_ref[...]
    for n in range(n_chunks):
        ca = slice(n * col_chunk, (n + 1) * col_chunk)
        bb = _dot(c, cp_ref[:, ca])
        m_ref[:, ca] = (ga_ref[:, ca] + gb_ref[:, ca] * bb).astype(_BF16)
    merged = m_ref[...]
    for n in range(n_chunks):
        ca = slice(n * col_chunk, (n + 1) * col_chunk)
        o_ref[:, ca] = x_ref[:, ca] + _dot(merged, wo_ref[:, ca])


def _mix_out(x, g, w_in, gate_col, b_gate, mixed, glu, conv_dw, conv_b, ln_g, ln_b, pool_w_proj,
             conv_w_proj, w_out, *, seq, tm=256):
    m, d = x.shape
    pool_width = mixed.shape[1]
    n_taps, conv_width = conv_dw.shape
    assert n_taps - 1 <= HALO and seq % tm == 0 and tm % (SUBLANES * ROW_STRIDE) == 0
    halo_blocks = tm // HALO
    n_lane_blocks = conv_width // LANES
    col_chunk = 512
    gate_block = math.gcd(gate_col, d)
    n_gate_blocks = 2 * d // gate_block
    assert gate_block % col_chunk == 0 and gate_col + 2 * d == w_in.shape[1]
    assert d % col_chunk == 0 and n_lane_blocks == 2 * (d // col_chunk)

    return pl.pallas_call(
        functools.partial(_mix_out_kernel, tiles_per_seq=seq // tm, col_chunk=col_chunk, ln_rows=32),
        grid=(m // tm,),
        in_specs=[
            pl.BlockSpec((tm, d), lambda i: (i, 0)),
            _resident((1, d)),
            [pl.BlockSpec((d, gate_block), lambda i, k=k: (0, gate_col // gate_block + k),
                          pipeline_mode=pl.Buffered(1)) for k in range(n_gate_blocks)],
            _resident(b_gate.shape),
            pl.BlockSpec((tm, pool_width), lambda i: (i, 0)),
            pl.BlockSpec((tm, conv_width), lambda i: (i, 0)),
            pl.BlockSpec((HALO, conv_width), lambda i: (jnp.maximum(i * halo_blocks - 1, 0), 0)),
            _resident((n_taps, n_lane_blocks, LANES)),
            _resident((n_lane_blocks, LANES)),
            _resident((1, conv_width)),
            _resident((1, conv_width)),
            _resident(pool_w_proj.shape),
            _resident(conv_w_proj.shape),
            _resident(w_out.shape),
        ],
        out_specs=pl.BlockSpec((tm, d), lambda i: (i, 0)),
        out_shape=jax.ShapeDtypeStruct((m, d), _F32),
        scratch_shapes=[
            pltpu.VMEM((tm, d), _BF16),
            pltpu.VMEM((n_lane_blocks, HALO + tm, LANES), _F32),
            pltpu.VMEM((n_lane_blocks, tm, LANES), _F32),
            pltpu.VMEM((tm, conv_width), _BF16),
            pltpu.VMEM((tm, d), _F32),
            pltpu.VMEM((tm, d), _F32),
            pltpu.VMEM((tm, d), _BF16),
        ],
        compiler_params=pltpu.CompilerParams(
            dimension_semantics=("parallel",), vmem_limit_bytes=VMEM_LIMIT_BYTES),
        name="mix_out",
    )(x, g, [w_in] * n_gate_blocks, b_gate, mixed, glu, glu, conv_dw.reshape(n_taps, n_lane_blocks, LANES),
      conv_b.reshape(n_lane_blocks, LANES), ln_g, ln_b, pool_w_proj, conv_w_proj, w_out)


def kernel(x, ffn1_norm, ffn1_w_in, ffn1_w_out, mix_norm, w_in, b_in, pool_w_grp, pool_scale,
           pool_w_proj, conv_dw, conv_b, conv_ln_g, conv_ln_b, conv_w_proj, w_out, ffn2_norm,
           ffn2_w_in, ffn2_w_out, final_norm):
    batch, seq, d = x.shape
    depth = ffn1_norm.shape[0]
    if depth == 0:
        raise ValueError("depth must be positive")
    pool_width = pool_w_proj.shape[1]
    conv_width = conv_w_proj.shape[1]
    n_pc = pool_width + 2 * conv_width

    def row(v):
        return v.reshape(1, -1).astype(_F32)

    xf = x.reshape(batch * seq, d)
    for l in range(depth):
        xf = _ffn(xf, row(ffn1_norm[l]), ffn1_w_in[l], ffn1_w_out[l])
        w_in_bf16 = w_in[l].astype(_BF16)
        mixed, glu = _mix_in(
            xf, row(mix_norm[l]), w_in_bf16, row(b_in[l][:n_pc]),
            pool_w_grp[l].astype(_BF16), row(pool_scale[l]), seq=seq)
        xf = _mix_out(
            xf, row(mix_norm[l]), w_in_bf16, n_pc, row(b_in[l][n_pc:]), mixed, glu,
            conv_dw[l], conv_b[l], row(conv_ln_g[l]), row(conv_ln_b[l]),
            pool_w_proj[l].astype(_BF16), conv_w_proj[l].astype(_BF16), w_out[l].astype(_BF16),
            seq=seq)
        final_g = row(final_norm) if l == depth - 1 else None
        xf = _ffn(xf, row(ffn2_norm[l]), ffn2_w_in[l], ffn2_w_out[l], final_g)
    return xf.reshape(batch, seq, d)
```

```python
import functools
import math

import jax
import jax.numpy as jnp
from jax import lax
from jax.experimental import pallas as pl
from jax.experimental.pallas import tpu as pltpu

EPS = 1e-6
FFN_RESIDUAL_WEIGHT = 0.5
POOL_WINDOWS = (2, 4, 8, 16)

HALO = 32
SUBLANES = 8
LANES = 128
ROW_STRIDE = 4
VMEM_LIMIT_BYTES = 58 * 1024 * 1024

_F32 = jnp.float32
_BF16 = jnp.bfloat16


def _rmsnorm(x, g):
    ms = jnp.mean(x * x, axis=-1, keepdims=True)
    return (x * lax.rsqrt(ms + EPS)) * g


def _dot(a, b):
    return jnp.dot(a, b, preferred_element_type=_F32)


def _resident(shape):
    return pl.BlockSpec(shape, lambda *_: (0,) * len(shape), pipeline_mode=pl.Buffered(1))


def _ffn_body(j, n_j, x_ref, g_ref, fg_ref, wg_ref, wu_ref, wo_ref, o_ref, h_ref,
              *, row_chunk, col_chunk, first_rows):
    tm, d = x_ref.shape
    n_row_chunks = tm // row_chunk

    def chunk_update(h, rows, first=False, last=False):
        acts = []
        for k in range(wg_ref.shape[0]):
            gate = _dot(h, wg_ref[k])
            up = _dot(h, wu_ref[k])
            acts.append(((gate * jax.nn.sigmoid(gate)) * up).astype(_BF16))
        act = acts[0] if len(acts) == 1 else jnp.concatenate(acts, axis=1)
        for n in range(d // col_chunk):
            cs = slice(n * col_chunk, (n + 1) * col_chunk)
            y = _dot(act, wo_ref[:, cs])
            if not first:
                y = o_ref[rows, cs] + y
            if last:
                y = x_ref[rows, cs] + FFN_RESIDUAL_WEIGHT * y
            o_ref[rows, cs] = y

    @pl.when(j == 0)
    def _():
        for r0 in range(0, tm, first_rows):
            rows = slice(r0, r0 + first_rows)
            h = _rmsnorm(x_ref[rows, :], g_ref[...]).astype(_BF16)
            h_ref[rows, :] = h
            chunk_update(h, rows, first=True)

    @pl.when(jnp.logical_and(j > 0, j < n_j - 1))
    def _():
        chunk_update(h_ref[...], slice(None))

    @pl.when(j == n_j - 1)
    def _():
        chunk_update(h_ref[...], slice(None), last=True)
        if fg_ref is not None:
            def body(r, carry):
                r0 = pl.multiple_of(r * row_chunk, row_chunk)
                rows = pl.ds(r0, row_chunk)
                o_ref[rows, :] = _rmsnorm(o_ref[rows, :], fg_ref[...])
                return carry

            lax.fori_loop(0, n_row_chunks, body, 0)


def _ffn_head_kernel(x_ref, g_ref, wg32_ref, wu32_ref, wo32_ref, *rest, final_norm, **chunks):
    fg_ref = rest[0] if final_norm else None
    o_ref, wg_ref, wu_ref, wo_ref, h_ref = rest[1:] if final_norm else rest
    wg_ref[0] = wg32_ref[...].astype(_BF16)
    wu_ref[0] = wu32_ref[...].astype(_BF16)
    wo_ref[...] = wo32_ref[...].astype(_BF16)
    _ffn_body(pl.program_id(0), pl.num_programs(0), x_ref, g_ref, fg_ref, wg_ref, wu_ref, wo_ref,
              o_ref, h_ref, **chunks)


def _ffn_tail_kernel(x_ref, y0_ref, g_ref, wg_ref, wu_ref, wo_ref, *rest, final_norm, **chunks):
    fg_ref = rest[0] if final_norm else None
    o_ref, h_ref = rest[1:] if final_norm else rest
    i, j = pl.program_id(0), pl.program_id(1)
    copy_cols = y0_ref.shape[1]

    @pl.when(i == 0)
    def _():
        for k in range(o_ref.shape[1] // copy_cols):
            @pl.when(j == k)
            def _(k=k):
                o_ref[:, k * copy_cols:(k + 1) * copy_cols] = y0_ref[...]

    @pl.when(i > 0)
    def _():
        _ffn_body(j, pl.num_programs(1), x_ref, g_ref, fg_ref, wg_ref, wu_ref, wo_ref, o_ref, h_ref,
                  **chunks)


def _ffn(x, g, w_in, w_out, final_g=None, *, tm=1024, tf=512, tf_head=256):
    m, d = x.shape
    f = w_out.shape[0]
    final_norm = final_g is not None
    chunks = dict(row_chunk=128, col_chunk=512, first_rows=256)
    extra_specs = [pl.BlockSpec((1, d), lambda *_: (0, 0))] if final_norm else []
    extra_args = [final_g] if final_norm else []
    suffix = "_final" if final_norm else ""

    n_head = f // tf_head
    assert f % tf == 0 and tf % tf_head == 0 and f // tf >= 2 and n_head >= 2
    y, wg, wu, wo = pl.pallas_call(
        functools.partial(_ffn_head_kernel, final_norm=final_norm, **chunks),
        grid=(n_head,),
        in_specs=[
            pl.BlockSpec((tm, d), lambda j: (0, 0), pipeline_mode=pl.Buffered(1)),
            pl.BlockSpec((1, d), lambda j: (0, 0)),
            pl.BlockSpec((d, tf_head), lambda j: (0, j)),
            pl.BlockSpec((d, tf_head), lambda j: (0, j + n_head)),
            pl.BlockSpec((tf_head, d), lambda j: (j, 0)),
        ] + extra_specs,
        out_specs=[
            pl.BlockSpec((tm, d), lambda j: (0, 0)),
            pl.BlockSpec((1, d, tf_head), lambda j: (j, 0, 0)),
            pl.BlockSpec((1, d, tf_head), lambda j: (j, 0, 0)),
            pl.BlockSpec((tf_head, d), lambda j: (j, 0)),
        ],
        out_shape=[
            jax.ShapeDtypeStruct((tm, d), _F32),
            jax.ShapeDtypeStruct((n_head, d, tf_head), _BF16),
            jax.ShapeDtypeStruct((n_head, d, tf_head), _BF16),
            jax.ShapeDtypeStruct((f, d), _BF16),
        ],
        scratch_shapes=[pltpu.VMEM((tm, d), _BF16)],
        compiler_params=pltpu.CompilerParams(
            dimension_semantics=("arbitrary",), vmem_limit_bytes=VMEM_LIMIT_BYTES),
        name="ffn_head" + suffix,
    )(x, g, w_in, w_in, w_out, *extra_args)

    n_f = f // tf
    copy_cols = 256
    n_copy = d // copy_cols
    assert n_copy <= n_f

    def chunk(i, j):
        return jnp.where(i == 0, 0, j)

    return pl.pallas_call(
        functools.partial(_ffn_tail_kernel, final_norm=final_norm, **chunks),
        grid=(m // tm, n_f),
        in_specs=[
            pl.BlockSpec((tm, d), lambda i, j: (i, 0)),
            pl.BlockSpec((tm, copy_cols),
                         lambda i, j: (0, jnp.where(i == 0, jnp.minimum(j, n_copy - 1), n_copy - 1))),
            pl.BlockSpec((1, d), lambda i, j: (0, 0)),
            pl.BlockSpec((tf // tf_head, d, tf_head), lambda i, j: (chunk(i, j), 0, 0)),
            pl.BlockSpec((tf // tf_head, d, tf_head), lambda i, j: (chunk(i, j), 0, 0)),
            pl.BlockSpec((tf, d), lambda i, j: (chunk(i, j), 0)),
        ] + extra_specs,
        out_specs=pl.BlockSpec((tm, d), lambda i, j: (i, 0)),
        out_shape=jax.ShapeDtypeStruct((m, d), _F32),
        scratch_shapes=[pltpu.VMEM((tm, d), _BF16)],
        compiler_params=pltpu.CompilerParams(
            dimension_semantics=("parallel", "arbitrary"), vmem_limit_bytes=VMEM_LIMIT_BYTES),
        name="ffn" + suffix,
    )(x, y, g, wg, wu, wo, *extra_args)


def _mix_in_kernel(x_ref, halo_ref, g_ref, w_ref, b_ref, wgrp_ref, pscale_ref, mixed_ref, glu_ref,
                   h_ref, u_ref, s_ref, t_ref, *, tiles_per_seq, norm_rows):
    i = pl.program_id(0)
    tm = x_ref.shape[0]
    pool_width = mixed_ref.shape[1]
    conv_width = glu_ref.shape[1]
    group = pool_width // len(POOL_WINDOWS)
    tile_in_seq = i % tiles_per_seq

    h_ref[0:HALO, :] = _rmsnorm(halo_ref[...], g_ref[...]).astype(_BF16)
    for r0 in range(0, tm, norm_rows):
        rows = slice(HALO + r0, HALO + r0 + norm_rows)
        h_ref[rows, :] = _rmsnorm(x_ref[r0:r0 + norm_rows, :], g_ref[...]).astype(_BF16)
        if r0 == 0:
            rows = slice(0, HALO + norm_rows)
        u = _dot(h_ref[rows, :], w_ref[:, 0:pool_width]) + b_ref[:, 0:pool_width]
        if r0 == 0:
            u_ref[0:HALO, :] = jnp.where(tile_in_seq == 0, 0.0, u[0:HALO])
            u = u[HALO:]
        u_ref[HALO + r0:HALO + r0 + norm_rows, :] = u

    pos = tile_in_seq * tm + lax.broadcasted_iota(jnp.int32, (tm, 1), 0)

    def pool_group(gi, window):
        cs = slice(gi * group, (gi + 1) * group)
        src, shift, lo, level = u_ref, 1, 8, 0
        while shift < window:
            dst = (s_ref, t_ref)[level % 2]
            dst[lo:, cs] = src[lo:, cs] + src[lo - shift:HALO + tm - shift, cs]
            src, shift, lo, level = dst, shift * 2, lo + 8, level + 1
        inv_count = 1.0 / jnp.minimum(pos + 1, window).astype(_F32)
        pooled = src[HALO:, cs] * inv_count - u_ref[HALO:, cs]
        mixed = _dot(pooled.astype(_BF16), wgrp_ref[gi])
        mixed_ref[:, cs] = (mixed * pscale_ref[:, cs]).astype(_BF16)

    h = h_ref[HALO:, :]
    glu_chunk = conv_width // len(POOL_WINDOWS)
    for gi, window in enumerate(POOL_WINDOWS):
        c0 = gi * glu_chunk
        ca = slice(pool_width + c0, pool_width + c0 + glu_chunk)
        cg = slice(pool_width + conv_width + c0, pool_width + conv_width + c0 + glu_chunk)
        za = _dot(h, w_ref[:, ca]) + b_ref[:, ca]
        zg = _dot(h, w_ref[:, cg]) + b_ref[:, cg]
        glu_ref[:, c0:c0 + glu_chunk] = za * jax.nn.sigmoid(zg)
        pool_group(gi, window)


def _mix_in(x, g, w_in, b_pc, w_grp, pool_scale, *, seq, tm=512):
    m, d = x.shape
    n_groups, group, _ = w_grp.shape
    pool_width = n_groups * group
    n_pc = b_pc.shape[1]
    conv_width = (n_pc - pool_width) // 2
    assert max(POOL_WINDOWS) <= HALO and seq % tm == 0 and tm % HALO == 0
    halo_blocks = tm // HALO

    return pl.pallas_call(
        functools.partial(_mix_in_kernel, tiles_per_seq=seq // tm, norm_rows=256),
        grid=(m // tm,),
        in_specs=[
            pl.BlockSpec((tm, d), lambda i: (i, 0)),
            pl.BlockSpec((HALO, d), lambda i: (jnp.maximum(i * halo_blocks - 1, 0), 0)),
            _resident((1, d)),
            _resident((d, n_pc)),
            _resident(b_pc.shape),
            _resident(w_grp.shape),
            _resident((1, pool_width)),
        ],
        out_specs=[
            pl.BlockSpec((tm, pool_width), lambda i: (i, 0)),
            pl.BlockSpec((tm, conv_width), lambda i: (i, 0)),
        ],
        out_shape=[
            jax.ShapeDtypeStruct((m, pool_width), _BF16),
            jax.ShapeDtypeStruct((m, conv_width), _F32),
        ],
        scratch_shapes=[
            pltpu.VMEM((HALO + tm, d), _BF16),
            pltpu.VMEM((HALO + tm, pool_width), _F32),
            pltpu.VMEM((HALO + tm, pool_width), _F32),
            pltpu.VMEM((HALO + tm, pool_width), _F32),
        ],
        compiler_params=pltpu.CompilerParams(
            dimension_semantics=("parallel",), vmem_limit_bytes=VMEM_LIMIT_BYTES),
        name="mix_in",
    )(x, x, g, w_in, b_pc, w_grp, pool_scale)


def _mix_out_kernel(x_ref, g_ref, wgate_refs, bgate_ref, mixed_ref, glu_ref, ghalo_ref, dw_ref,
                    cb_ref, lng_ref, lnb_ref, pp_ref, cp_ref, wo_ref, o_ref,
                    m_ref, gl_ref, conv_ref, c_ref, ga_ref, gb_ref, h_ref,
                    *, tiles_per_seq, col_chunk, ln_rows):
    i = pl.program_id(0)
    tm, d = x_ref.shape
    conv_width = glu_ref.shape[1]
    n_lane_blocks = conv_width // LANES
    n_taps = dw_ref.shape[0]
    first_in_seq = (i % tiles_per_seq) == 0

    for cb in range(n_lane_blocks):
        cs = slice(cb * LANES, (cb + 1) * LANES)
        gl_ref[cb, 0:HALO, :] = jnp.where(first_in_seq, 0.0, ghalo_ref[:, cs])
        gl_ref[cb, HALO:, :] = glu_ref[:, cs]

    block_rows = SUBLANES * ROW_STRIDE
    half_rows = tm // 2

    def conv_rows(cb, r0):
        bias = cb_ref[pl.ds(cb, SUBLANES, stride=0), :]
        blocks = range(half_rows // block_rows)
        accs = [[bias] * ROW_STRIDE for _ in blocks]
        taps = {}
        for q in range(n_taps + ROW_STRIDE - 1):
            if q < n_taps:
                taps[q] = dw_ref[q, pl.ds(cb, SUBLANES, stride=0), :]
            for blk in blocks:
                lo = r0 + blk * block_rows + HALO - (n_taps - 1) + q
                v = gl_ref[cb, pl.ds(lo, SUBLANES, stride=ROW_STRIDE), :]
                for ph in range(ROW_STRIDE):
                    if 0 <= q - ph < n_taps:
                        accs[blk][ph] = accs[blk][ph] + taps[q - ph] * v
        for blk in blocks:
            for ph in range(ROW_STRIDE):
                rows = pl.ds(r0 + blk * block_rows + ph, SUBLANES, stride=ROW_STRIDE)
                conv_ref[cb, rows, :] = accs[blk][ph]

    def conv_branch(r_lo):
        for cb in range(n_lane_blocks):
            conv_rows(cb, r_lo)
        for r0 in range(r_lo, r_lo + half_rows, ln_rows):
            y = [conv_ref[cb, r0:r0 + ln_rows, :] for cb in range(n_lane_blocks)]
            mu = jnp.sum(sum(y), axis=-1, keepdims=True) * (1.0 / conv_width)
            yc = [v - mu for v in y]
            var = jnp.sum(sum(v * v for v in yc), axis=-1, keepdims=True) * (1.0 / conv_width)
            inv = lax.rsqrt(var + EPS)
            for cb in range(n_lane_blocks):
                cs = slice(cb * LANES, (cb + 1) * LANES)
                z = (yc[cb] * inv) * lng_ref[:, cs] + lnb_ref[:, cs]
                c_ref[r0:r0 + ln_rows, cs] = (z * jax.nn.sigmoid(z)).astype(_BF16)
        rows = slice(r_lo, r_lo + half_rows)
        c = c_ref[rows, :]
        for n in range(n_chunks):
            ca = slice(n * col_chunk, (n + 1) * col_chunk)
            gb_ref[rows, ca] = gb_ref[rows, ca] * _dot(c, cp_ref[:, ca])

    h_ref[...] = _rmsnorm(x_ref[...], g_ref[...]).astype(_BF16)
    gate_block = wgate_refs[0].shape[1]

    def gate_weight(col):
        ref, lo = wgate_refs[col // gate_block], col % gate_block
        return ref[:, lo:lo + col_chunk]

    n_chunks = d // col_chunk
    for n in range(n_chunks):
        ca = slice(n * col_chunk, (n + 1) * col_chunk)
        cb = slice(d + n * col_chunk, d + (n + 1) * col_chunk)
        gb_ref[:, ca] = jax.nn.sigmoid(_dot(h_ref[...], gate_weight(cb.start)) + bgate_ref[:, cb])
    conv_branch(0)
    for n in range(n_chunks):
        ca = slice(n * col_chunk, (n + 1) * col_chunk)
        gate_a = jax.nn.sigmoid(_dot(h_ref[...], gate_weight(ca.start)) + bgate_ref[:, ca])
        ga_ref[:, ca] = gate_a * _dot(mixed_ref[...], pp_ref[:, ca])
    conv_branch(half_rows)

    for n in range(n_chunks):
        ca = slice(n * col_chunk, (n + 1) * col_chunk)
        m_ref[:, ca] = (ga_ref[:, ca] + gb_ref[:, ca]).astype(_BF16)
    merged = m_ref[...]
    for n in range(n_chunks):
        ca = slice(n * col_chunk, (n + 1) * col_chunk)
        o_ref[:, ca] = x_ref[:, ca] + _dot(merged, wo_ref[:, ca])


def _mix_out(x, g, w_in, gate_col, b_gate, mixed, glu, conv_dw, conv_b, ln_g, ln_b, pool_w_proj,
             conv_w_proj, w_out, *, seq, tm=256):
    m, d = x.shape
    pool_width = mixed.shape[1]
    n_taps, conv_width = conv_dw.shape
    assert n_taps - 1 <= HALO and seq % tm == 0 and tm % (2 * SUBLANES * ROW_STRIDE) == 0
    halo_blocks = tm // HALO
    n_lane_blocks = conv_width // LANES
    col_chunk = 512
    gate_block = math.gcd(gate_col, d)
    n_gate_blocks = 2 * d // gate_block
    assert gate_block % col_chunk == 0 and gate_col + 2 * d == w_in.shape[1]
    assert d % col_chunk == 0

    return pl.pallas_call(
        functools.partial(_mix_out_kernel, tiles_per_seq=seq // tm, col_chunk=col_chunk, ln_rows=32),
        grid=(m // tm,),
        in_specs=[
            pl.BlockSpec((tm, d), lambda i: (i, 0)),
            _resident((1, d)),
            [pl.BlockSpec((d, gate_block), lambda i, k=k: (0, gate_col // gate_block + k),
                          pipeline_mode=pl.Buffered(1)) for k in range(n_gate_blocks)],
            _resident(b_gate.shape),
            pl.BlockSpec((tm, pool_width), lambda i: (i, 0)),
            pl.BlockSpec((tm, conv_width), lambda i: (i, 0)),
            pl.BlockSpec((HALO, conv_width), lambda i: (jnp.maximum(i * halo_blocks - 1, 0), 0)),
            _resident((n_taps, n_lane_blocks, LANES)),
            _resident((n_lane_blocks, LANES)),
            _resident((1, conv_width)),
            _resident((1, conv_width)),
            _resident(pool_w_proj.shape),
            _resident(conv_w_proj.shape),
            _resident(w_out.shape),
        ],
        out_specs=pl.BlockSpec((tm, d), lambda i: (i, 0)),
        out_shape=jax.ShapeDtypeStruct((m, d), _F32),
        scratch_shapes=[
            pltpu.VMEM((tm, d), _BF16),
            pltpu.VMEM((n_lane_blocks, HALO + tm, LANES), _F32),
            pltpu.VMEM((n_lane_blocks, tm, LANES), _F32),
            pltpu.VMEM((tm, conv_width), _BF16),
            pltpu.VMEM((tm, d), _F32),
            pltpu.VMEM((tm, d), _F32),
            pltpu.VMEM((tm, d), _BF16),
        ],
        compiler_params=pltpu.CompilerParams(
            dimension_semantics=("parallel",), vmem_limit_bytes=VMEM_LIMIT_BYTES),
        name="mix_out",
    )(x, g, [w_in] * n_gate_blocks, b_gate, mixed, glu, glu, conv_dw.reshape(n_taps, n_lane_blocks, LANES),
      conv_b.reshape(n_lane_blocks, LANES), ln_g, ln_b, pool_w_proj, conv_w_proj, w_out)


def kernel(x, ffn1_norm, ffn1_w_in, ffn1_w_out, mix_norm, w_in, b_in, pool_w_grp, pool_scale,
           pool_w_proj, conv_dw, conv_b, conv_ln_g, conv_ln_b, conv_w_proj, w_out, ffn2_norm,
           ffn2_w_in, ffn2_w_out, final_norm):
    batch, seq, d = x.shape
    depth = ffn1_norm.shape[0]
    if depth == 0:
        raise ValueError("depth must be positive")
    pool_width = pool_w_proj.shape[1]
    conv_width = conv_w_proj.shape[1]
    n_pc = pool_width + 2 * conv_width

    def row(v):
        return v.reshape(1, -1).astype(_F32)

    xf = x.reshape(batch * seq, d)
    for l in range(depth):
        xf = _ffn(xf, row(ffn1_norm[l]), ffn1_w_in[l], ffn1_w_out[l])
        w_in_bf16 = w_in[l].astype(_BF16)
        mixed, glu = _mix_in(
            xf, row(mix_norm[l]), w_in_bf16, row(b_in[l][:n_pc]),
            pool_w_grp[l].astype(_BF16), row(pool_scale[l]), seq=seq)
        xf = _mix_out(
            xf, row(mix_norm[l]), w_in_bf16, n_pc, row(b_in[l][n_pc:]), mixed, glu,
            conv_dw[l], conv_b[l], row(conv_ln_g[l]), row(conv_ln_b[l]),
            pool_w_proj[l].astype(_BF16), conv_w_proj[l].astype(_BF16), w_out[l].astype(_BF16),
            seq=seq)
        final_g = row(final_norm) if l == depth - 1 else None
        xf = _ffn(xf, row(ffn2_norm[l]), ffn2_w_in[l], ffn2_w_out[l], final_g)
    return xf.reshape(batch, seq, d)
```

```python
import functools
import math

import jax
import jax.numpy as jnp
from jax import lax
from jax.experimental import pallas as pl
from jax.experimental.pallas import tpu as pltpu

EPS = 1e-6
FFN_RESIDUAL_WEIGHT = 0.5
POOL_WINDOWS = (2, 4, 8, 16)

HALO = 32
SUBLANES = 8
LANES = 128
ROW_STRIDE = 4
CONV_ACCUMULATORS = 16
VMEM_LIMIT_BYTES = 58 * 1024 * 1024

_F32 = jnp.float32
_BF16 = jnp.bfloat16


def _rmsnorm(x, g):
    ms = jnp.mean(x * x, axis=-1, keepdims=True)
    return (x * lax.rsqrt(ms + EPS)) * g


def _dot(a, b):
    return jnp.dot(a, b, preferred_element_type=_F32)


def _resident(shape):
    return pl.BlockSpec(shape, lambda *_: (0,) * len(shape), pipeline_mode=pl.Buffered(1))


def _ffn_body(j, n_j, x_ref, g_ref, fg_ref, wg_ref, wu_ref, wo_ref, o_ref, h_ref,
              *, row_chunk, col_chunk, first_rows):
    tm, d = x_ref.shape
    n_row_chunks = tm // row_chunk

    def chunk_update(h, rows, first=False, last=False):
        acts = []
        for k in range(wg_ref.shape[0]):
            gate = _dot(h, wg_ref[k])
            up = _dot(h, wu_ref[k])
            acts.append(((gate * jax.nn.sigmoid(gate)) * up).astype(_BF16))
        act = acts[0] if len(acts) == 1 else jnp.concatenate(acts, axis=1)
        for n in range(d // col_chunk):
            cs = slice(n * col_chunk, (n + 1) * col_chunk)
            y = _dot(act, wo_ref[:, cs])
            if not first:
                y = o_ref[rows, cs] + y
            if last:
                y = x_ref[rows, cs] + FFN_RESIDUAL_WEIGHT * y
            o_ref[rows, cs] = y

    @pl.when(j == 0)
    def _():
        for r0 in range(0, tm, first_rows):
            rows = slice(r0, r0 + first_rows)
            h = _rmsnorm(x_ref[rows, :], g_ref[...]).astype(_BF16)
            h_ref[rows, :] = h
            chunk_update(h, rows, first=True)

    @pl.when(jnp.logical_and(j > 0, j < n_j - 1))
    def _():
        chunk_update(h_ref[...], slice(None))

    @pl.when(j == n_j - 1)
    def _():
        chunk_update(h_ref[...], slice(None), last=True)
        if fg_ref is not None:
            def body(r, carry):
                r0 = pl.multiple_of(r * row_chunk, row_chunk)
                rows = pl.ds(r0, row_chunk)
                o_ref[rows, :] = _rmsnorm(o_ref[rows, :], fg_ref[...])
                return carry

            lax.fori_loop(0, n_row_chunks, body, 0)


def _ffn_head_kernel(x_ref, g_ref, wg32_ref, wu32_ref, wo32_ref, *rest, final_norm, **chunks):
    fg_ref = rest[0] if final_norm else None
    o_ref, wg_ref, wu_ref, wo_ref, h_ref = rest[1:] if final_norm else rest
    wg_ref[0] = wg32_ref[...].astype(_BF16)
    wu_ref[0] = wu32_ref[...].astype(_BF16)
    wo_ref[...] = wo32_ref[...].astype(_BF16)
    _ffn_body(pl.program_id(0), pl.num_programs(0), x_ref, g_ref, fg_ref, wg_ref, wu_ref, wo_ref,
              o_ref, h_ref, **chunks)


def _ffn_tail_kernel(x_ref, y0_ref, g_ref, wg_ref, wu_ref, wo_ref, *rest, final_norm, **chunks):
    fg_ref = rest[0] if final_norm else None
    o_ref, h_ref = rest[1:] if final_norm else rest
    i, j = pl.program_id(0), pl.program_id(1)
    copy_cols = y0_ref.shape[1]

    @pl.when(i == 0)
    def _():
        for k in range(o_ref.shape[1] // copy_cols):
            @pl.when(j == k)
            def _(k=k):
                o_ref[:, k * copy_cols:(k + 1) * copy_cols] = y0_ref[...]

    @pl.when(i > 0)
    def _():
        _ffn_body(j, pl.num_programs(1), x_ref, g_ref, fg_ref, wg_ref, wu_ref, wo_ref, o_ref, h_ref,
                  **chunks)


def _ffn(x, g, w_in, w_out, final_g=None, *, tm=1024, tf=512, tf_head=256):
    m, d = x.shape
    f = w_out.shape[0]
    final_norm = final_g is not None
    chunks = dict(row_chunk=128, col_chunk=512, first_rows=256)
    extra_specs = [pl.BlockSpec((1, d), lambda *_: (0, 0))] if final_norm else []
    extra_args = [final_g] if final_norm else []
    suffix = "_final" if final_norm else ""

    n_head = f // tf_head
    assert f % tf == 0 and tf % tf_head == 0 and f // tf >= 2 and n_head >= 2
    y, wg, wu, wo = pl.pallas_call(
        functools.partial(_ffn_head_kernel, final_norm=final_norm, **chunks),
        grid=(n_head,),
        in_specs=[
            pl.BlockSpec((tm, d), lambda j: (0, 0), pipeline_mode=pl.Buffered(1)),
            pl.BlockSpec((1, d), lambda j: (0, 0)),
            pl.BlockSpec((d, tf_head), lambda j: (0, j)),
            pl.BlockSpec((d, tf_head), lambda j: (0, j + n_head)),
            pl.BlockSpec((tf_head, d), lambda j: (j, 0)),
        ] + extra_specs,
        out_specs=[
            pl.BlockSpec((tm, d), lambda j: (0, 0)),
            pl.BlockSpec((1, d, tf_head), lambda j: (j, 0, 0)),
            pl.BlockSpec((1, d, tf_head), lambda j: (j, 0, 0)),
            pl.BlockSpec((tf_head, d), lambda j: (j, 0)),
        ],
        out_shape=[
            jax.ShapeDtypeStruct((tm, d), _F32),
            jax.ShapeDtypeStruct((n_head, d, tf_head), _BF16),
            jax.ShapeDtypeStruct((n_head, d, tf_head), _BF16),
            jax.ShapeDtypeStruct((f, d), _BF16),
        ],
        scratch_shapes=[pltpu.VMEM((tm, d), _BF16)],
        compiler_params=pltpu.CompilerParams(
            dimension_semantics=("arbitrary",), vmem_limit_bytes=VMEM_LIMIT_BYTES),
        name="ffn_head" + suffix,
    )(x, g, w_in, w_in, w_out, *extra_args)

    n_f = f // tf
    copy_cols = 256
    n_copy = d // copy_cols
    assert n_copy <= n_f

    def chunk(i, j):
        return jnp.where(i == 0, 0, j)

    return pl.pallas_call(
        functools.partial(_ffn_tail_kernel, final_norm=final_norm, **chunks),
        grid=(m // tm, n_f),
        in_specs=[
            pl.BlockSpec((tm, d), lambda i, j: (i, 0)),
            pl.BlockSpec((tm, copy_cols),
                         lambda i, j: (0, jnp.where(i == 0, jnp.minimum(j, n_copy - 1), n_copy - 1))),
            pl.BlockSpec((1, d), lambda i, j: (0, 0)),
            pl.BlockSpec((tf // tf_head, d, tf_head), lambda i, j: (chunk(i, j), 0, 0)),
            pl.BlockSpec((tf // tf_head, d, tf_head), lambda i, j: (chunk(i, j), 0, 0)),
            pl.BlockSpec((tf, d), lambda i, j: (chunk(i, j), 0)),
        ] + extra_specs,
        out_specs=pl.BlockSpec((tm, d), lambda i, j: (i, 0)),
        out_shape=jax.ShapeDtypeStruct((m, d), _F32),
        scratch_shapes=[pltpu.VMEM((tm, d), _BF16)],
        compiler_params=pltpu.CompilerParams(
            dimension_semantics=("parallel", "arbitrary"), vmem_limit_bytes=VMEM_LIMIT_BYTES),
        name="ffn" + suffix,
    )(x, y, g, wg, wu, wo, *extra_args)


def _mix_in_kernel(x_ref, halo_ref, g_ref, w_ref, b_ref, wgrp_ref, pscale_ref, mixed_ref, glu_ref,
                   h_ref, u_ref, s_ref, t_ref, *, tiles_per_seq, norm_rows):
    i = pl.program_id(0)
    tm = x_ref.shape[0]
    pool_width = mixed_ref.shape[1]
    conv_width = glu_ref.shape[1]
    group = pool_width // len(POOL_WINDOWS)
    tile_in_seq = i % tiles_per_seq

    h_ref[0:HALO, :] = _rmsnorm(halo_ref[...], g_ref[...]).astype(_BF16)
    for r0 in range(0, tm, norm_rows):
        rows = slice(HALO + r0, HALO + r0 + norm_rows)
        h_ref[rows, :] = _rmsnorm(x_ref[r0:r0 + norm_rows, :], g_ref[...]).astype(_BF16)
        if r0 == 0:
            rows = slice(0, HALO + norm_rows)
        u = _dot(h_ref[rows, :], w_ref[:, 0:pool_width]) + b_ref[:, 0:pool_width]
        if r0 == 0:
            u_ref[0:HALO, :] = jnp.where(tile_in_seq == 0, 0.0, u[0:HALO])
            u = u[HALO:]
        u_ref[HALO + r0:HALO + r0 + norm_rows, :] = u

    pos = tile_in_seq * tm + lax.broadcasted_iota(jnp.int32, (tm, 1), 0)

    def pool_group(gi, window):
        cs = slice(gi * group, (gi + 1) * group)
        src, shift, lo, level = u_ref, 1, 8, 0
        while shift < window:
            dst = (s_ref, t_ref)[level % 2]
            dst[lo:, cs] = src[lo:, cs] + src[lo - shift:HALO + tm - shift, cs]
            src, shift, lo, level = dst, shift * 2, lo + 8, level + 1
        inv_count = 1.0 / jnp.minimum(pos + 1, window).astype(_F32)
        pooled = src[HALO:, cs] * inv_count - u_ref[HALO:, cs]
        mixed = _dot(pooled.astype(_BF16), wgrp_ref[gi])
        mixed_ref[:, cs] = (mixed * pscale_ref[:, cs]).astype(_BF16)

    h = h_ref[HALO:, :]
    glu_chunk = conv_width // len(POOL_WINDOWS)
    for gi, window in enumerate(POOL_WINDOWS):
        c0 = gi * glu_chunk
        ca = slice(pool_width + c0, pool_width + c0 + glu_chunk)
        cg = slice(pool_width + conv_width + c0, pool_width + conv_width + c0 + glu_chunk)
        za = _dot(h, w_ref[:, ca]) + b_ref[:, ca]
        zg = _dot(h, w_ref[:, cg]) + b_ref[:, cg]
        glu_ref[:, c0:c0 + glu_chunk] = za * jax.nn.sigmoid(zg)
        pool_group(gi, window)


def _mix_in(x, g, w_in, b_pc, w_grp, pool_scale, *, seq, tm=512):
    m, d = x.shape
    n_groups, group, _ = w_grp.shape
    pool_width = n_groups * group
    n_pc = b_pc.shape[1]
    conv_width = (n_pc - pool_width) // 2
    assert max(POOL_WINDOWS) <= HALO and seq % tm == 0 and tm % HALO == 0
    halo_blocks = tm // HALO

    return pl.pallas_call(
        functools.partial(_mix_in_kernel, tiles_per_seq=seq // tm, norm_rows=256),
        grid=(m // tm,),
        in_specs=[
            pl.BlockSpec((tm, d), lambda i: (i, 0)),
            pl.BlockSpec((HALO, d), lambda i: (jnp.maximum(i * halo_blocks - 1, 0), 0)),
            _resident((1, d)),
            _resident((d, n_pc)),
            _resident(b_pc.shape),
            _resident(w_grp.shape),
            _resident((1, pool_width)),
        ],
        out_specs=[
            pl.BlockSpec((tm, pool_width), lambda i: (i, 0)),
            pl.BlockSpec((tm, conv_width), lambda i: (i, 0)),
        ],
        out_shape=[
            jax.ShapeDtypeStruct((m, pool_width), _BF16),
            jax.ShapeDtypeStruct((m, conv_width), _F32),
        ],
        scratch_shapes=[
            pltpu.VMEM((HALO + tm, d), _BF16),
            pltpu.VMEM((HALO + tm, pool_width), _F32),
            pltpu.VMEM((HALO + tm, pool_width), _F32),
            pltpu.VMEM((HALO + tm, pool_width), _F32),
        ],
        compiler_params=pltpu.CompilerParams(
            dimension_semantics=("parallel",), vmem_limit_bytes=VMEM_LIMIT_BYTES),
        name="mix_in",
    )(x, x, g, w_in, b_pc, w_grp, pool_scale)


def _mix_out_kernel(x_ref, g_ref, wgate_refs, bgate_ref, mixed_ref, glu_ref, ghalo_ref, dw_ref,
                    cb_ref, lng_ref, lnb_ref, pp_ref, cp_ref, wo_ref, o_ref,
                    m_ref, gl_ref, conv_ref, c_ref, ga_ref, gb_ref, h_ref,
                    *, tiles_per_seq, col_chunk, ln_rows):
    i = pl.program_id(0)
    tm, d = x_ref.shape
    conv_width = glu_ref.shape[1]
    n_lane_blocks = conv_width // LANES
    n_taps = dw_ref.shape[0]
    first_in_seq = (i % tiles_per_seq) == 0

    for cb in range(n_lane_blocks):
        cs = slice(cb * LANES, (cb + 1) * LANES)
        gl_ref[cb, 0:HALO, :] = jnp.where(first_in_seq, 0.0, ghalo_ref[:, cs])
        gl_ref[cb, HALO:, :] = glu_ref[:, cs]

    rows_per_group = min(tm, CONV_ACCUMULATORS * SUBLANES)

    def conv_lane_block(cb):
        bias = cb_ref[pl.ds(cb, SUBLANES, stride=0), :]
        block_rows = SUBLANES * ROW_STRIDE
        for r0 in range(0, tm, rows_per_group):
            blocks = range(rows_per_group // block_rows)
            accs = [[bias] * ROW_STRIDE for _ in blocks]
            taps = {}
            for q in range(n_taps + ROW_STRIDE - 1):
                if q < n_taps:
                    taps[q] = dw_ref[q, pl.ds(cb, SUBLANES, stride=0), :]
                for blk in blocks:
                    lo = r0 + blk * block_rows + HALO - (n_taps - 1) + q
                    v = gl_ref[cb, pl.ds(lo, SUBLANES, stride=ROW_STRIDE), :]
                    for ph in range(ROW_STRIDE):
                        if 0 <= q - ph < n_taps:
                            accs[blk][ph] = accs[blk][ph] + taps[q - ph] * v
            for blk in blocks:
                for ph in range(ROW_STRIDE):
                    rows = pl.ds(r0 + blk * block_rows + ph, SUBLANES, stride=ROW_STRIDE)
                    conv_ref[cb, rows, :] = accs[blk][ph]

    h_ref[...] = _rmsnorm(x_ref[...], g_ref[...]).astype(_BF16)
    gate_block = wgate_refs[0].shape[1]

    def gate_weight(col):
        ref, lo = wgate_refs[col // gate_block], col % gate_block
        return ref[:, lo:lo + col_chunk]

    n_chunks = d // col_chunk
    convs_per_chunk = n_lane_blocks // n_chunks
    for n in range(n_chunks):
        ca = slice(n * col_chunk, (n + 1) * col_chunk)
        cb = slice(d + n * col_chunk, d + (n + 1) * col_chunk)
        for k in range(convs_per_chunk):
            conv_lane_block(n * convs_per_chunk + k)
        h = h_ref[...]
        gate_a = jax.nn.sigmoid(_dot(h, gate_weight(ca.start)) + bgate_ref[:, ca])
        ga_ref[:, ca] = gate_a * _dot(mixed_ref[...], pp_ref[:, ca])
        gb_ref[:, ca] = jax.nn.sigmoid(_dot(h, gate_weight(cb.start)) + bgate_ref[:, cb])

    for r0 in range(0, tm, ln_rows):
        y = [conv_ref[cb, r0:r0 + ln_rows, :] for cb in range(n_lane_blocks)]
        mu = jnp.sum(sum(y), axis=-1, keepdims=True) * (1.0 / conv_width)
        yc = [v - mu for v in y]
        var = jnp.sum(sum(v * v for v in yc), axis=-1, keepdims=True) * (1.0 / conv_width)
        inv = lax.rsqrt(var + EPS)
        for cb in range(n_lane_blocks):
            cs = slice(cb * LANES, (cb + 1) * LANES)
            z = (yc[cb] * inv) * lng_ref[:, cs] + lnb_ref[:, cs]
            c_ref[r0:r0 + ln_rows, cs] = (z * jax.nn.sigmoid(z)).astype(_BF16)

    c = c_ref[...]
    for n in range(n_chunks):
        ca = slice(n * col_chunk, (n + 1) * col_chunk)
        bb = _dot(c, cp_ref[:, ca])
        m_ref[:, ca] = (ga_ref[:, ca] + gb_ref[:, ca] * bb).astype(_BF16)
    merged = m_ref[...]
    for n in range(n_chunks):
        ca = slice(n * col_chunk, (n + 1) * col_chunk)
        o_ref[:, ca] = x_ref[:, ca] + _dot(merged, wo_ref[:, ca])


def _mix_out(x, g, w_in, gate_col, b_gate, mixed, glu, conv_dw, conv_b, ln_g, ln_b, pool_w_proj,
             conv_w_proj, w_out, *, seq, tm=256):
    m, d = x.shape
    pool_width = mixed.shape[1]
    n_taps, conv_width = conv_dw.shape
    assert n_taps - 1 <= HALO and seq % tm == 0 and tm % (SUBLANES * ROW_STRIDE) == 0
    halo_blocks = tm // HALO
    n_lane_blocks = conv_width // LANES
    col_chunk = 512
    gate_block = math.gcd(gate_col, d)
    n_gate_blocks = 2 * d // gate_block
    assert gate_block % col_chunk == 0 and gate_col + 2 * d == w_in.shape[1]
    assert d % col_chunk == 0 and n_lane_blocks == 2 * (d // col_chunk)

    return pl.pallas_call(
        functools.partial(_mix_out_kernel, tiles_per_seq=seq // tm, col_chunk=col_chunk, ln_rows=32),
        grid=(m // tm,),
        in_specs=[
            pl.BlockSpec((tm, d), lambda i: (i, 0)),
            _resident((1, d)),
            [pl.BlockSpec((d, gate_block), lambda i, k=k: (0, gate_col // gate_block + k),
                          pipeline_mode=pl.Buffered(1)) for k in range(n_gate_blocks)],
            _resident(b_gate.shape),
            pl.BlockSpec((tm, pool_width), lambda i: (i, 0)),
            pl.BlockSpec((tm, conv_width), lambda i: (i, 0)),
            pl.BlockSpec((HALO, conv_width), lambda i: (jnp.maximum(i * halo_blocks - 1, 0), 0)),
            _resident((n_taps, n_lane_blocks, LANES)),
            _resident((n_lane_blocks, LANES)),
            _resident((1, conv_width)),
            _resident((1, conv_width)),
            _resident(pool_w_proj.shape),
            _resident(conv_w_proj.shape),
            _resident(w_out.shape),
        ],
        out_specs=pl.BlockSpec((tm, d), lambda i: (i, 0)),
        out_shape=jax.ShapeDtypeStruct((m, d), _F32),
        scratch_shapes=[
            pltpu.VMEM((tm, d), _BF16),
            pltpu.VMEM((n_lane_blocks, HALO + tm, LANES), _F32),
            pltpu.VMEM((n_lane_blocks, tm, LANES), _F32),
            pltpu.VMEM((tm, conv_width), _BF16),
            pltpu.VMEM((tm, d), _F32),
            pltpu.VMEM((tm, d), _F32),
            pltpu.VMEM((tm, d), _BF16),
        ],
        compiler_params=pltpu.CompilerParams(
            dimension_semantics=("parallel",), vmem_limit_bytes=VMEM_LIMIT_BYTES),
        name="mix_out",
    )(x, g, [w_in] * n_gate_blocks, b_gate, mixed, glu, glu, conv_dw.reshape(n_taps, n_lane_blocks, LANES),
      conv_b.reshape(n_lane_blocks, LANES), ln_g, ln_b, pool_w_proj, conv_w_proj, w_out)


def kernel(x, ffn1_norm, ffn1_w_in, ffn1_w_out, mix_norm, w_in, b_in, pool_w_grp, pool_scale,
           pool_w_proj, conv_dw, conv_b, conv_ln_g, conv_ln_b, conv_w_proj, w_out, ffn2_norm,
           ffn2_w_in, ffn2_w_out, final_norm):
    batch, seq, d = x.shape
    depth = ffn1_norm.shape[0]
    if depth == 0:
        raise ValueError("depth must be positive")
    pool_width = pool_w_proj.shape[1]
    conv_width = conv_w_proj.shape[1]
    n_pc = pool_width + 2 * conv_width

    def row(v):
        return v.reshape(1, -1).astype(_F32)

    xf = x.reshape(batch * seq, d)
    for l in range(depth):
        xf = _ffn(xf, row(ffn1_norm[l]), ffn1_w_in[l], ffn1_w_out[l])
        w_in_bf16 = w_in[l].astype(_BF16)
        mixed, glu = _mix_in(
            xf, row(mix_norm[l]), w_in_bf16, row(b_in[l][:n_pc]),
            pool_w_grp[l].astype(_BF16), row(pool_scale[l]), seq=seq)
        xf = _mix_out(
            xf, row(mix_norm[l]), w_in_bf16, n_pc, row(b_in[l][n_pc:]), mixed, glu,
            conv_dw[l], conv_b[l], row(conv_ln_g[l]), row(conv_ln_b[l]),
            pool_w_proj[l].astype(_BF16), conv_w_proj[l].astype(_BF16), w_out[l].astype(_BF16),
            seq=seq)
        final_g = row(final_norm) if l == depth - 1 else None
        xf = _ffn(xf, row(ffn2_norm[l]), ffn2_w_in[l], ffn2_w_out[l], final_g)
    return xf.reshape(batch, seq, d)
```

```python
import functools
import math

import jax
import jax.numpy as jnp
from jax import lax
from jax.experimental import pallas as pl
from jax.experimental.pallas import tpu as pltpu

EPS = 1e-6
FFN_RESIDUAL_WEIGHT = 0.5
POOL_WINDOWS = (2, 4, 8, 16)

HALO = 32
SUBLANES = 8
LANES = 128
ROW_STRIDE = 4
CONV_ACCUMULATORS = 16
VMEM_LIMIT_BYTES = 58 * 1024 * 1024

_F32 = jnp.float32
_BF16 = jnp.bfloat16


def _rmsnorm(x, g):
    ms = jnp.mean(x * x, axis=-1, keepdims=True)
    return (x * lax.rsqrt(ms + EPS)) * g


def _dot(a, b):
    return jnp.dot(a, b, preferred_element_type=_F32)


def _resident(shape):
    return pl.BlockSpec(shape, lambda *_: (0,) * len(shape), pipeline_mode=pl.Buffered(1))


def _ffn_body(j, n_j, x_ref, g_ref, fg_ref, wg_ref, wu_ref, wo_ref, o_ref, h_ref,
              *, row_chunk, col_chunk, first_rows):
    tm, d = x_ref.shape
    n_row_chunks = tm // row_chunk

    def chunk_update(h, rows, first=False, last=False):
        acts = []
        for k in range(wg_ref.shape[0]):
            gate = _dot(h, wg_ref[k])
            up = _dot(h, wu_ref[k])
            acts.append(((gate * jax.nn.sigmoid(gate)) * up).astype(_BF16))
        act = acts[0] if len(acts) == 1 else jnp.concatenate(acts, axis=1)
        for n in range(d // col_chunk):
            cs = slice(n * col_chunk, (n + 1) * col_chunk)
            y = _dot(act, wo_ref[:, cs])
            if not first:
                y = o_ref[rows, cs] + y
            if last:
                y = x_ref[rows, cs] + FFN_RESIDUAL_WEIGHT * y
            o_ref[rows, cs] = y

    @pl.when(j == 0)
    def _():
        for r0 in range(0, tm, first_rows):
            rows = slice(r0, r0 + first_rows)
            h = _rmsnorm(x_ref[rows, :], g_ref[...]).astype(_BF16)
            h_ref[rows, :] = h
            chunk_update(h, rows, first=True)

    @pl.when(jnp.logical_and(j > 0, j < n_j - 1))
    def _():
        chunk_update(h_ref[...], slice(None))

    @pl.when(j == n_j - 1)
    def _():
        chunk_update(h_ref[...], slice(None), last=True)
        if fg_ref is not None:
            def body(r, carry):
                r0 = pl.multiple_of(r * row_chunk, row_chunk)
                rows = pl.ds(r0, row_chunk)
                o_ref[rows, :] = _rmsnorm(o_ref[rows, :], fg_ref[...])
                return carry

            lax.fori_loop(0, n_row_chunks, body, 0)


def _ffn_head_kernel(x_ref, g_ref, wg32_ref, wu32_ref, wo32_ref, *rest, final_norm, **chunks):
    fg_ref = rest[0] if final_norm else None
    o_ref, wg_ref, wu_ref, wo_ref, h_ref = rest[1:] if final_norm else rest
    wg_ref[0] = wg32_ref[...].astype(_BF16)
    wu_ref[0] = wu32_ref[...].astype(_BF16)
    wo_ref[...] = wo32_ref[...].astype(_BF16)
    _ffn_body(pl.program_id(0), pl.num_programs(0), x_ref, g_ref, fg_ref, wg_ref, wu_ref, wo_ref,
              o_ref, h_ref, **chunks)


def _ffn_tail_kernel(x_ref, y0_ref, g_ref, wg_ref, wu_ref, wo_ref, *rest, final_norm, **chunks):
    fg_ref = rest[0] if final_norm else None
    o_ref, h_ref = rest[1:] if final_norm else rest
    i, j = pl.program_id(0), pl.program_id(1)
    copy_cols = y0_ref.shape[1]

    @pl.when(i == 0)
    def _():
        for k in range(o_ref.shape[1] // copy_cols):
            @pl.when(j == k)
            def _(k=k):
                o_ref[:, k * copy_cols:(k + 1) * copy_cols] = y0_ref[...]

    @pl.when(i > 0)
    def _():
        _ffn_body(j, pl.num_programs(1), x_ref, g_ref, fg_ref, wg_ref, wu_ref, wo_ref, o_ref, h_ref,
                  **chunks)


def _ffn(x, g, w_in, w_out, final_g=None, *, tm=1024, tf=512, tf_head=256):
    m, d = x.shape
    f = w_out.shape[0]
    final_norm = final_g is not None
    chunks = dict(row_chunk=128, col_chunk=512, first_rows=256)
    extra_specs = [pl.BlockSpec((1, d), lambda *_: (0, 0))] if final_norm else []
    extra_args = [final_g] if final_norm else []
    suffix = "_final" if final_norm else ""

    n_head = f // tf_head
    assert f % tf == 0 and tf % tf_head == 0 and f // tf >= 2 and n_head >= 2
    y, wg, wu, wo = pl.pallas_call(
        functools.partial(_ffn_head_kernel, final_norm=final_norm, **chunks),
        grid=(n_head,),
        in_specs=[
            pl.BlockSpec((tm, d), lambda j: (0, 0), pipeline_mode=pl.Buffered(1)),
            pl.BlockSpec((1, d), lambda j: (0, 0)),
            pl.BlockSpec((d, tf_head), lambda j: (0, j)),
            pl.BlockSpec((d, tf_head), lambda j: (0, j + n_head)),
            pl.BlockSpec((tf_head, d), lambda j: (j, 0)),
        ] + extra_specs,
        out_specs=[
            pl.BlockSpec((tm, d), lambda j: (0, 0)),
            pl.BlockSpec((1, d, tf_head), lambda j: (j, 0, 0)),
            pl.BlockSpec((1, d, tf_head), lambda j: (j, 0, 0)),
            pl.BlockSpec((tf_head, d), lambda j: (j, 0)),
        ],
        out_shape=[
            jax.ShapeDtypeStruct((tm, d), _F32),
            jax.ShapeDtypeStruct((n_head, d, tf_head), _BF16),
            jax.ShapeDtypeStruct((n_head, d, tf_head), _BF16),
            jax.ShapeDtypeStruct((f, d), _BF16),
        ],
        scratch_shapes=[pltpu.VMEM((tm, d), _BF16)],
        compiler_params=pltpu.CompilerParams(
            dimension_semantics=("arbitrary",), vmem_limit_bytes=VMEM_LIMIT_BYTES),
        name="ffn_head" + suffix,
    )(x, g, w_in, w_in, w_out, *extra_args)

    n_f = f // tf
    copy_cols = 256
    n_copy = d // copy_cols
    assert n_copy <= n_f

    def chunk(i, j):
        return jnp.where(i == 0, 0, j)

    return pl.pallas_call(
        functools.partial(_ffn_tail_kernel, final_norm=final_norm, **chunks),
        grid=(m // tm, n_f),
        in_specs=[
            pl.BlockSpec((tm, d), lambda i, j: (i, 0)),
            pl.BlockSpec((tm, copy_cols),
                         lambda i, j: (0, jnp.where(i == 0, jnp.minimum(j, n_copy - 1), n_copy - 1))),
            pl.BlockSpec((1, d), lambda i, j: (0, 0)),
            pl.BlockSpec((tf // tf_head, d, tf_head), lambda i, j: (chunk(i, j), 0, 0)),
            pl.BlockSpec((tf // tf_head, d, tf_head), lambda i, j: (chunk(i, j), 0, 0)),
            pl.BlockSpec((tf, d), lambda i, j: (chunk(i, j), 0)),
        ] + extra_specs,
        out_specs=pl.BlockSpec((tm, d), lambda i, j: (i, 0)),
        out_shape=jax.ShapeDtypeStruct((m, d), _F32),
        scratch_shapes=[pltpu.VMEM((tm, d), _BF16)],
        compiler_params=pltpu.CompilerParams(
            dimension_semantics=("parallel", "arbitrary"), vmem_limit_bytes=VMEM_LIMIT_BYTES),
        name="ffn" + suffix,
    )(x, y, g, wg, wu, wo, *extra_args)


def _mix_in_kernel(x_ref, halo_ref, g_ref, w_ref, b_ref, wgrp_ref, pscale_ref, mixed_ref, glu_ref,
                   h_ref, u_ref, s_ref, t_ref, *, tiles_per_seq, norm_rows):
    i = pl.program_id(0)
    tm = x_ref.shape[0]
    pool_width = mixed_ref.shape[1]
    conv_width = glu_ref.shape[1]
    group = pool_width // len(POOL_WINDOWS)
    tile_in_seq = i % tiles_per_seq

    h_ref[0:HALO, :] = _rmsnorm(halo_ref[...], g_ref[...]).astype(_BF16)
    for r0 in range(0, tm, norm_rows):
        rows = slice(HALO + r0, HALO + r0 + norm_rows)
        h_ref[rows, :] = _rmsnorm(x_ref[r0:r0 + norm_rows, :], g_ref[...]).astype(_BF16)
        if r0 == 0:
            rows = slice(0, HALO + norm_rows)
        u = _dot(h_ref[rows, :], w_ref[:, 0:pool_width]) + b_ref[:, 0:pool_width]
        if r0 == 0:
            u_ref[0:HALO, :] = jnp.where(tile_in_seq == 0, 0.0, u[0:HALO])
            u = u[HALO:]
        u_ref[HALO + r0:HALO + r0 + norm_rows, :] = u

    pos = tile_in_seq * tm + lax.broadcasted_iota(jnp.int32, (tm, 1), 0)

    def pool_group(gi, window):
        cs = slice(gi * group, (gi + 1) * group)
        src, shift, lo, level = u_ref, 1, 8, 0
        while shift < window:
            dst = (s_ref, t_ref)[level % 2]
            dst[lo:, cs] = src[lo:, cs] + src[lo - shift:HALO + tm - shift, cs]
            src, shift, lo, level = dst, shift * 2, lo + 8, level + 1
        inv_count = 1.0 / jnp.minimum(pos + 1, window).astype(_F32)
        pooled = src[HALO:, cs] * inv_count - u_ref[HALO:, cs]
        mixed = _dot(pooled.astype(_BF16), wgrp_ref[gi])
        mixed_ref[:, cs] = (mixed * pscale_ref[:, cs]).astype(_BF16)

    h = h_ref[HALO:, :]
    glu_chunk = conv_width // len(POOL_WINDOWS)
    for gi, window in enumerate(POOL_WINDOWS):
        c0 = gi * glu_chunk
        ca = slice(pool_width + c0, pool_width + c0 + glu_chunk)
        cg = slice(pool_width + conv_width + c0, pool_width + conv_width + c0 + glu_chunk)
        za = _dot(h, w_ref[:, ca]) + b_ref[:, ca]
        zg = _dot(h, w_ref[:, cg]) + b_ref[:, cg]
        glu_ref[:, c0:c0 + glu_chunk] = za * jax.nn.sigmoid(zg)
        pool_group(gi, window)


def _mix_in(x, g, w_in, b_pc, w_grp, pool_scale, *, seq, tm=512):
    m, d = x.shape
    n_groups, group, _ = w_grp.shape
    pool_width = n_groups * group
    n_pc = b_pc.shape[1]
    conv_width = (n_pc - pool_width) // 2
    assert max(POOL_WINDOWS) <= HALO and seq % tm == 0 and tm % HALO == 0
    halo_blocks = tm // HALO

    return pl.pallas_call(
        functools.partial(_mix_in_kernel, tiles_per_seq=seq // tm, norm_rows=256),
        grid=(m // tm,),
        in_specs=[
            pl.BlockSpec((tm, d), lambda i: (i, 0)),
            pl.BlockSpec((HALO, d), lambda i: (jnp.maximum(i * halo_blocks - 1, 0), 0)),
            _resident((1, d)),
            _resident((d, n_pc)),
            _resident(b_pc.shape),
            _resident(w_grp.shape),
            _resident((1, pool_width)),
        ],
        out_specs=[
            pl.BlockSpec((tm, pool_width), lambda i: (i, 0)),
            pl.BlockSpec((tm, conv_width), lambda i: (i, 0)),
        ],
        out_shape=[
            jax.ShapeDtypeStruct((m, pool_width), _BF16),
            jax.ShapeDtypeStruct((m, conv_width), _F32),
        ],
        scratch_shapes=[
            pltpu.VMEM((HALO + tm, d), _BF16),
            pltpu.VMEM((HALO + tm, pool_width), _F32),
            pltpu.VMEM((HALO + tm, pool_width), _F32),
            pltpu.VMEM((HALO + tm, pool_width), _F32),
        ],
        compiler_params=pltpu.CompilerParams(
            dimension_semantics=("parallel",), vmem_limit_bytes=VMEM_LIMIT_BYTES),
        name="mix_in",
    )(x, x, g, w_in, b_pc, w_grp, pool_scale)


def _mix_out_kernel(x_ref, g_ref, wgate_refs, bgate_ref, mixed_ref, glu_ref, ghalo_ref, dw_ref,
                    cb_ref, lng_ref, lnb_ref, pp_ref, cp_ref, wo_ref, o_ref,
                    m_ref, gl_ref, conv_ref, c_ref, ga_ref, gb_ref, h_ref,
                    *, tiles_per_seq, col_chunk, ln_rows):
    i = pl.program_id(0)
    tm, d = x_ref.shape
    conv_width = glu_ref.shape[1]
    n_lane_blocks = conv_width // LANES
    n_taps = dw_ref.shape[0]
    first_in_seq = (i % tiles_per_seq) == 0

    for cb in range(n_lane_blocks):
        cs = slice(cb * LANES, (cb + 1) * LANES)
        gl_ref[cb, 0:HALO, :] = jnp.where(first_in_seq, 0.0, ghalo_ref[:, cs])
        gl_ref[cb, HALO:, :] = glu_ref[:, cs]

    rows_per_group = min(tm, CONV_ACCUMULATORS * SUBLANES)

    def conv_lane_block(cb):
        bias = cb_ref[pl.ds(cb, SUBLANES, stride=0), :]
        bits = []
        block_rows = SUBLANES * ROW_STRIDE
        for r0 in range(0, tm, rows_per_group):
            blocks = range(rows_per_group // block_rows)
            accs = [[bias] * ROW_STRIDE for _ in blocks]
            taps = {}
            for q in range(n_taps + ROW_STRIDE - 1):
                if q < n_taps:
                    taps[q] = dw_ref[q, pl.ds(cb, SUBLANES, stride=0), :]
                for blk in blocks:
                    lo = r0 + blk * block_rows + HALO - (n_taps - 1) + q
                    v = gl_ref[cb, pl.ds(lo, SUBLANES, stride=ROW_STRIDE), :]
                    for ph in range(ROW_STRIDE):
                        if 0 <= q - ph < n_taps:
                            accs[blk][ph] = accs[blk][ph] + taps[q - ph] * v
            for blk in blocks:
                for ph in range(ROW_STRIDE):
                    rows = pl.ds(r0 + blk * block_rows + ph, SUBLANES, stride=ROW_STRIDE)
                    conv_ref[cb, rows, :] = accs[blk][ph]
                    bits.append(lax.bitcast_convert_type(accs[blk][ph], jnp.uint32))
        word = functools.reduce(lambda a, b: a | b, bits)
        zero = lax.shift_right_logical(lax.shift_right_logical(word, jnp.uint32(16)), jnp.uint32(16))
        return lax.bitcast_convert_type(zero, _F32)[0:1, :]

    h_ref[...] = _rmsnorm(x_ref[...], g_ref[...]).astype(_BF16)
    gate_block = wgate_refs[0].shape[1]

    def gate_weight(col):
        ref, lo = wgate_refs[col // gate_block], col % gate_block
        return ref[:, lo:lo + col_chunk]

    n_chunks = d // col_chunk
    convs_per_chunk = n_lane_blocks // n_chunks
    for n in range(n_chunks):
        ca = slice(n * col_chunk, (n + 1) * col_chunk)
        cb = slice(d + n * col_chunk, d + (n + 1) * col_chunk)
        zeros = [jnp.tile(conv_lane_block(n * convs_per_chunk + k), (1, col_chunk // LANES))
                 for k in range(convs_per_chunk)]
        h = h_ref[...]
        gate_a = jax.nn.sigmoid(_dot(h, gate_weight(ca.start)) + (bgate_ref[:, ca] + zeros[0]))
        ga_ref[:, ca] = gate_a * _dot(mixed_ref[...], pp_ref[:, ca])
        gb_ref[:, ca] = jax.nn.sigmoid(_dot(h, gate_weight(cb.start)) + (bgate_ref[:, cb] + zeros[-1]))

    for r0 in range(0, tm, ln_rows):
        y = [conv_ref[cb, r0:r0 + ln_rows, :] for cb in range(n_lane_blocks)]
        mu = jnp.sum(sum(y), axis=-1, keepdims=True) * (1.0 / conv_width)
        yc = [v - mu for v in y]
        var = jnp.sum(sum(v * v for v in yc), axis=-1, keepdims=True) * (1.0 / conv_width)
        inv = lax.rsqrt(var + EPS)
        for cb in range(n_lane_blocks):
            cs = slice(cb * LANES, (cb + 1) * LANES)
            z = (yc[cb] * inv) * lng_ref[:, cs] + lnb_ref[:, cs]
            c_ref[r0:r0 + ln_rows, cs] = (z * jax.nn.sigmoid(z)).astype(_BF16)

    c = c_ref[...]
    for n in range(n_chunks):
        ca = slice(n * col_chunk, (n + 1) * col_chunk)
        bb = _dot(c, cp_ref[:, ca])
        m_ref[:, ca] = (ga_ref[:, ca] + gb_ref[:, ca] * bb).astype(_BF16)
    merged = m_ref[...]
    for n in range(n_chunks):
        ca = slice(n * col_chunk, (n + 1) * col_chunk)
        o_ref[:, ca] = x_ref[:, ca] + _dot(merged, wo_ref[:, ca])


def _mix_out(x, g, w_in, gate_col, b_gate, mixed, glu, conv_dw, conv_b, ln_g, ln_b, pool_w_proj,
             conv_w_proj, w_out, *, seq, tm=256):
    m, d = x.shape
    pool_width = mixed.shape[1]
    n_taps, conv_width = conv_dw.shape
    assert n_taps - 1 <= HALO and seq % tm == 0 and tm % (SUBLANES * ROW_STRIDE) == 0
    halo_blocks = tm // HALO
    n_lane_blocks = conv_width // LANES
    col_chunk = 512
    gate_block = math.gcd(gate_col, d)
    n_gate_blocks = 2 * d // gate_block
    assert gate_block % col_chunk == 0 and gate_col + 2 * d == w_in.shape[1]
    assert d % col_chunk == 0 and n_lane_blocks == 2 * (d // col_chunk)

    return pl.pallas_call(
        functools.partial(_mix_out_kernel, tiles_per_seq=seq // tm, col_chunk=col_chunk, ln_rows=32),
        grid=(m // tm,),
        in_specs=[
            pl.BlockSpec((tm, d), lambda i: (i, 0)),
            _resident((1, d)),
            [pl.BlockSpec((d, gate_block), lambda i, k=k: (0, gate_col // gate_block + k),
                          pipeline_mode=pl.Buffered(1)) for k in range(n_gate_blocks)],
            _resident(b_gate.shape),
            pl.BlockSpec((tm, pool_width), lambda i: (i, 0)),
            pl.BlockSpec((tm, conv_width), lambda i: (i, 0)),
            pl.BlockSpec((HALO, conv_width), lambda i: (jnp.maximum(i * halo_blocks - 1, 0), 0)),
            _resident((n_taps, n_lane_blocks, LANES)),
            _resident((n_lane_blocks, LANES)),
            _resident((1, conv_width)),
            _resident((1, conv_width)),
            _resident(pool_w_proj.shape),
            _resident(conv_w_proj.shape),
            _resident(w_out.shape),
        ],
        out_specs=pl.BlockSpec((tm, d), lambda i: (i, 0)),
        out_shape=jax.ShapeDtypeStruct((m, d), _F32),
        scratch_shapes=[
            pltpu.VMEM((tm, d), _BF16),
            pltpu.VMEM((n_lane_blocks, HALO + tm, LANES), _F32),
            pltpu.VMEM((n_lane_blocks, tm, LANES), _F32),
            pltpu.VMEM((tm, conv_width), _BF16),
            pltpu.VMEM((tm, d), _F32),
            pltpu.VMEM((tm, d), _F32),
            pltpu.VMEM((tm, d), _BF16),
        ],
        compiler_params=pltpu.CompilerParams(
            dimension_semantics=("parallel",), vmem_limit_bytes=VMEM_LIMIT_BYTES),
        name="mix_out",
    )(x, g, [w_in] * n_gate_blocks, b_gate, mixed, glu, glu, conv_dw.reshape(n_taps, n_lane_blocks, LANES),
      conv_b.reshape(n_lane_blocks, LANES), ln_g, ln_b, pool_w_proj, conv_w_proj, w_out)


def kernel(x, ffn1_norm, ffn1_w_in, ffn1_w_out, mix_norm, w_in, b_in, pool_w_grp, pool_scale,
           pool_w_proj, conv_dw, conv_b, conv_ln_g, conv_ln_b, conv_w_proj, w_out, ffn2_norm,
           ffn2_w_in, ffn2_w_out, final_norm):
    batch, seq, d = x.shape
    depth = ffn1_norm.shape[0]
    if depth == 0:
        raise ValueError("depth must be positive")
    pool_width = pool_w_proj.shape[1]
    conv_width = conv_w_proj.shape[1]
    n_pc = pool_width + 2 * conv_width

    def row(v):
        return v.reshape(1, -1).astype(_F32)

    xf = x.reshape(batch * seq, d)
    for l in range(depth):
        xf = _ffn(xf, row(ffn1_norm[l]), ffn1_w_in[l], ffn1_w_out[l])
        w_in_bf16 = w_in[l].astype(_BF16)
        mixed, glu = _mix_in(
            xf, row(mix_norm[l]), w_in_bf16, row(b_in[l][:n_pc]),
            pool_w_grp[l].astype(_BF16), row(pool_scale[l]), seq=seq)
        xf = _mix_out(
            xf, row(mix_norm[l]), w_in_bf16, n_pc, row(b_in[l][n_pc:]), mixed, glu,
            conv_dw[l], conv_b[l], row(conv_ln_g[l]), row(conv_ln_b[l]),
            pool_w_proj[l].astype(_BF16), conv_w_proj[l].astype(_BF16), w_out[l].astype(_BF16),
            seq=seq)
        final_g = row(final_norm) if l == depth - 1 else None
        xf = _ffn(xf, row(ffn2_norm[l]), ffn2_w_in[l], ffn2_w_out[l], final_g)
    return xf.reshape(batch, seq, d)
```

```python
import functools
import math

import jax
import jax.numpy as jnp
from jax import lax
from jax.experimental import pallas as pl
from jax.experimental.pallas import tpu as pltpu

EPS = 1e-6
FFN_RESIDUAL_WEIGHT = 0.5
POOL_WINDOWS = (2, 4, 8, 16)

HALO = 32
SUBLANES = 8
LANES = 128
ROW_STRIDE = 2
CONV_ACCUMULATORS = 16
VMEM_LIMIT_BYTES = 58 * 1024 * 1024

_F32 = jnp.float32
_BF16 = jnp.bfloat16


def _rmsnorm(x, g):
    ms = jnp.mean(x * x, axis=-1, keepdims=True)
    return (x * lax.rsqrt(ms + EPS)) * g


def _dot(a, b):
    return jnp.dot(a, b, preferred_element_type=_F32)


def _resident(shape):
    return pl.BlockSpec(shape, lambda *_: (0,) * len(shape), pipeline_mode=pl.Buffered(1))


def _ffn_body(j, n_j, x_ref, g_ref, fg_ref, wg_ref, wu_ref, wo_ref, o_ref, h_ref,
              *, row_chunk, col_chunk, first_rows):
    tm, d = x_ref.shape
    n_row_chunks = tm // row_chunk

    def chunk_update(h, rows, first=False, last=False):
        acts = []
        for k in range(wg_ref.shape[0]):
            gate = _dot(h, wg_ref[k])
            up = _dot(h, wu_ref[k])
            acts.append(((gate * jax.nn.sigmoid(gate)) * up).astype(_BF16))
        act = acts[0] if len(acts) == 1 else jnp.concatenate(acts, axis=1)
        for n in range(d // col_chunk):
            cs = slice(n * col_chunk, (n + 1) * col_chunk)
            y = _dot(act, wo_ref[:, cs])
            if not first:
                y = o_ref[rows, cs] + y
            if last:
                y = x_ref[rows, cs] + FFN_RESIDUAL_WEIGHT * y
            o_ref[rows, cs] = y

    @pl.when(j == 0)
    def _():
        for r0 in range(0, tm, first_rows):
            rows = slice(r0, r0 + first_rows)
            h = _rmsnorm(x_ref[rows, :], g_ref[...]).astype(_BF16)
            h_ref[rows, :] = h
            chunk_update(h, rows, first=True)

    @pl.when(jnp.logical_and(j > 0, j < n_j - 1))
    def _():
        chunk_update(h_ref[...], slice(None))

    @pl.when(j == n_j - 1)
    def _():
        chunk_update(h_ref[...], slice(None), last=True)
        if fg_ref is not None:
            def body(r, carry):
                r0 = pl.multiple_of(r * row_chunk, row_chunk)
                rows = pl.ds(r0, row_chunk)
                o_ref[rows, :] = _rmsnorm(o_ref[rows, :], fg_ref[...])
                return carry

            lax.fori_loop(0, n_row_chunks, body, 0)


def _ffn_head_kernel(x_ref, g_ref, wg32_ref, wu32_ref, wo32_ref, *rest, final_norm, **chunks):
    fg_ref = rest[0] if final_norm else None
    o_ref, wg_ref, wu_ref, wo_ref, h_ref = rest[1:] if final_norm else rest
    wg_ref[0] = wg32_ref[...].astype(_BF16)
    wu_ref[0] = wu32_ref[...].astype(_BF16)
    wo_ref[...] = wo32_ref[...].astype(_BF16)
    _ffn_body(pl.program_id(0), pl.num_programs(0), x_ref, g_ref, fg_ref, wg_ref, wu_ref, wo_ref,
              o_ref, h_ref, **chunks)


def _ffn_tail_kernel(x_ref, y0_ref, g_ref, wg_ref, wu_ref, wo_ref, *rest, final_norm, **chunks):
    fg_ref = rest[0] if final_norm else None
    o_ref, h_ref = rest[1:] if final_norm else rest
    i, j = pl.program_id(0), pl.program_id(1)
    copy_cols = y0_ref.shape[1]

    @pl.when(i == 0)
    def _():
        for k in range(o_ref.shape[1] // copy_cols):
            @pl.when(j == k)
            def _(k=k):
                o_ref[:, k * copy_cols:(k + 1) * copy_cols] = y0_ref[...]

    @pl.when(i > 0)
    def _():
        _ffn_body(j, pl.num_programs(1), x_ref, g_ref, fg_ref, wg_ref, wu_ref, wo_ref, o_ref, h_ref,
                  **chunks)


def _ffn(x, g, w_in, w_out, final_g=None, *, tm=1024, tf=512, tf_head=256):
    m, d = x.shape
    f = w_out.shape[0]
    final_norm = final_g is not None
    chunks = dict(row_chunk=128, col_chunk=512, first_rows=256)
    extra_specs = [pl.BlockSpec((1, d), lambda *_: (0, 0))] if final_norm else []
    extra_args = [final_g] if final_norm else []
    suffix = "_final" if final_norm else ""

    n_head = f // tf_head
    assert f % tf == 0 and tf % tf_head == 0 and f // tf >= 2 and n_head >= 2
    y, wg, wu, wo = pl.pallas_call(
        functools.partial(_ffn_head_kernel, final_norm=final_norm, **chunks),
        grid=(n_head,),
        in_specs=[
            pl.BlockSpec((tm, d), lambda j: (0, 0), pipeline_mode=pl.Buffered(1)),
            pl.BlockSpec((1, d), lambda j: (0, 0)),
            pl.BlockSpec((d, tf_head), lambda j: (0, j)),
            pl.BlockSpec((d, tf_head), lambda j: (0, j + n_head)),
            pl.BlockSpec((tf_head, d), lambda j: (j, 0)),
        ] + extra_specs,
        out_specs=[
            pl.BlockSpec((tm, d), lambda j: (0, 0)),
            pl.BlockSpec((1, d, tf_head), lambda j: (j, 0, 0)),
            pl.BlockSpec((1, d, tf_head), lambda j: (j, 0, 0)),
            pl.BlockSpec((tf_head, d), lambda j: (j, 0)),
        ],
        out_shape=[
            jax.ShapeDtypeStruct((tm, d), _F32),
            jax.ShapeDtypeStruct((n_head, d, tf_head), _BF16),
            jax.ShapeDtypeStruct((n_head, d, tf_head), _BF16),
            jax.ShapeDtypeStruct((f, d), _BF16),
        ],
        scratch_shapes=[pltpu.VMEM((tm, d), _BF16)],
        compiler_params=pltpu.CompilerParams(
            dimension_semantics=("arbitrary",), vmem_limit_bytes=VMEM_LIMIT_BYTES),
        name="ffn_head" + suffix,
    )(x, g, w_in, w_in, w_out, *extra_args)

    n_f = f // tf
    copy_cols = 256
    n_copy = d // copy_cols
    assert n_copy <= n_f

    def chunk(i, j):
        return jnp.where(i == 0, 0, j)

    return pl.pallas_call(
        functools.partial(_ffn_tail_kernel, final_norm=final_norm, **chunks),
        grid=(m // tm, n_f),
        in_specs=[
            pl.BlockSpec((tm, d), lambda i, j: (i, 0)),
            pl.BlockSpec((tm, copy_cols),
                         lambda i, j: (0, jnp.where(i == 0, jnp.minimum(j, n_copy - 1), n_copy - 1))),
            pl.BlockSpec((1, d), lambda i, j: (0, 0)),
            pl.BlockSpec((tf // tf_head, d, tf_head), lambda i, j: (chunk(i, j), 0, 0)),
            pl.BlockSpec((tf // tf_head, d, tf_head), lambda i, j: (chunk(i, j), 0, 0)),
            pl.BlockSpec((tf, d), lambda i, j: (chunk(i, j), 0)),
        ] + extra_specs,
        out_specs=pl.BlockSpec((tm, d), lambda i, j: (i, 0)),
        out_shape=jax.ShapeDtypeStruct((m, d), _F32),
        scratch_shapes=[pltpu.VMEM((tm, d), _BF16)],
        compiler_params=pltpu.CompilerParams(
            dimension_semantics=("parallel", "arbitrary"), vmem_limit_bytes=VMEM_LIMIT_BYTES),
        name="ffn" + suffix,
    )(x, y, g, wg, wu, wo, *extra_args)


def _mix_in_kernel(x_ref, halo_ref, g_ref, w_ref, b_ref, wgrp_ref, pscale_ref, mixed_ref, glu_ref,
                   h_ref, u_ref, s_ref, t_ref, *, tiles_per_seq, norm_rows):
    i = pl.program_id(0)
    tm = x_ref.shape[0]
    pool_width = mixed_ref.shape[1]
    conv_width = glu_ref.shape[1]
    group = pool_width // len(POOL_WINDOWS)
    tile_in_seq = i % tiles_per_seq

    h_ref[0:HALO, :] = _rmsnorm(halo_ref[...], g_ref[...]).astype(_BF16)
    for r0 in range(0, tm, norm_rows):
        rows = slice(HALO + r0, HALO + r0 + norm_rows)
        h_ref[rows, :] = _rmsnorm(x_ref[r0:r0 + norm_rows, :], g_ref[...]).astype(_BF16)
        if r0 == 0:
            rows = slice(0, HALO + norm_rows)
        u = _dot(h_ref[rows, :], w_ref[:, 0:pool_width]) + b_ref[:, 0:pool_width]
        if r0 == 0:
            u_ref[0:HALO, :] = jnp.where(tile_in_seq == 0, 0.0, u[0:HALO])
            u = u[HALO:]
        u_ref[HALO + r0:HALO + r0 + norm_rows, :] = u

    pos = tile_in_seq * tm + lax.broadcasted_iota(jnp.int32, (tm, 1), 0)

    def pool_group(gi, window):
        cs = slice(gi * group, (gi + 1) * group)
        src, shift, lo, level = u_ref, 1, 8, 0
        while shift < window:
            dst = (s_ref, t_ref)[level % 2]
            dst[lo:, cs] = src[lo:, cs] + src[lo - shift:HALO + tm - shift, cs]
            src, shift, lo, level = dst, shift * 2, lo + 8, level + 1
        inv_count = 1.0 / jnp.minimum(pos + 1, window).astype(_F32)
        pooled = src[HALO:, cs] * inv_count - u_ref[HALO:, cs]
        mixed = _dot(pooled.astype(_BF16), wgrp_ref[gi])
        mixed_ref[:, cs] = (mixed * pscale_ref[:, cs]).astype(_BF16)

    h = h_ref[HALO:, :]
    glu_chunk = conv_width // len(POOL_WINDOWS)
    for gi, window in enumerate(POOL_WINDOWS):
        c0 = gi * glu_chunk
        ca = slice(pool_width + c0, pool_width + c0 + glu_chunk)
        cg = slice(pool_width + conv_width + c0, pool_width + conv_width + c0 + glu_chunk)
        za = _dot(h, w_ref[:, ca]) + b_ref[:, ca]
        zg = _dot(h, w_ref[:, cg]) + b_ref[:, cg]
        glu_ref[:, c0:c0 + glu_chunk] = za * jax.nn.sigmoid(zg)
        pool_group(gi, window)


def _mix_in(x, g, w_in, b_pc, w_grp, pool_scale, *, seq, tm=512):
    m, d = x.shape
    n_groups, group, _ = w_grp.shape
    pool_width = n_groups * group
    n_pc = b_pc.shape[1]
    conv_width = (n_pc - pool_width) // 2
    assert max(POOL_WINDOWS) <= HALO and seq % tm == 0 and tm % HALO == 0
    halo_blocks = tm // HALO

    return pl.pallas_call(
        functools.partial(_mix_in_kernel, tiles_per_seq=seq // tm, norm_rows=256),
        grid=(m // tm,),
        in_specs=[
            pl.BlockSpec((tm, d), lambda i: (i, 0)),
            pl.BlockSpec((HALO, d), lambda i: (jnp.maximum(i * halo_blocks - 1, 0), 0)),
            _resident((1, d)),
            _resident((d, n_pc)),
            _resident(b_pc.shape),
            _resident(w_grp.shape),
            _resident((1, pool_width)),
        ],
        out_specs=[
            pl.BlockSpec((tm, pool_width), lambda i: (i, 0)),
            pl.BlockSpec((tm, conv_width), lambda i: (i, 0)),
        ],
        out_shape=[
            jax.ShapeDtypeStruct((m, pool_width), _BF16),
            jax.ShapeDtypeStruct((m, conv_width), _F32),
        ],
        scratch_shapes=[
            pltpu.VMEM((HALO + tm, d), _BF16),
            pltpu.VMEM((HALO + tm, pool_width), _F32),
            pltpu.VMEM((HALO + tm, pool_width), _F32),
            pltpu.VMEM((HALO + tm, pool_width), _F32),
        ],
        compiler_params=pltpu.CompilerParams(
            dimension_semantics=("parallel",), vmem_limit_bytes=VMEM_LIMIT_BYTES),
        name="mix_in",
    )(x, x, g, w_in, b_pc, w_grp, pool_scale)


def _mix_out_kernel(x_ref, g_ref, wgate_refs, bgate_ref, mixed_ref, glu_ref, ghalo_ref, dw_ref,
                    cb_ref, lng_ref, lnb_ref, pp_ref, cp_ref, wo_ref, o_ref,
                    m_ref, gl_ref, conv_ref, c_ref, ga_ref, gb_ref, h_ref,
                    *, tiles_per_seq, col_chunk, ln_rows):
    i = pl.program_id(0)
    tm, d = x_ref.shape
    conv_width = glu_ref.shape[1]
    n_lane_blocks = conv_width // LANES
    n_taps = dw_ref.shape[0]
    first_in_seq = (i % tiles_per_seq) == 0

    for cb in range(n_lane_blocks):
        cs = slice(cb * LANES, (cb + 1) * LANES)
        gl_ref[cb, 0:HALO, :] = jnp.where(first_in_seq, 0.0, ghalo_ref[:, cs])
        gl_ref[cb, HALO:, :] = glu_ref[:, cs]

    rows_per_group = min(tm, CONV_ACCUMULATORS * SUBLANES)

    def conv_lane_block(cb):
        bias = cb_ref[pl.ds(cb, SUBLANES, stride=0), :]
        block_rows = SUBLANES * ROW_STRIDE
        for r0 in range(0, tm, rows_per_group):
            blocks = range(rows_per_group // block_rows)
            accs = [[bias] * ROW_STRIDE for _ in blocks]
            taps = {}
            for q in range(n_taps + ROW_STRIDE - 1):
                if q < n_taps:
                    taps[q] = dw_ref[q, pl.ds(cb, SUBLANES, stride=0), :]
                for blk in blocks:
                    lo = r0 + blk * block_rows + HALO - (n_taps - 1) + q
                    v = gl_ref[cb, pl.ds(lo, SUBLANES, stride=ROW_STRIDE), :]
                    for ph in range(ROW_STRIDE):
                        if 0 <= q - ph < n_taps:
                            accs[blk][ph] = accs[blk][ph] + taps[q - ph] * v
            for blk in blocks:
                for ph in range(ROW_STRIDE):
                    rows = pl.ds(r0 + blk * block_rows + ph, SUBLANES, stride=ROW_STRIDE)
                    conv_ref[cb, rows, :] = accs[blk][ph]

    h_ref[...] = _rmsnorm(x_ref[...], g_ref[...]).astype(_BF16)
    gate_block = wgate_refs[0].shape[1]

    def gate_weight(col):
        ref, lo = wgate_refs[col // gate_block], col % gate_block
        return ref[:, lo:lo + col_chunk]

    n_chunks = d // col_chunk
    convs_per_chunk = n_lane_blocks // n_chunks
    for n in range(n_chunks):
        ca = slice(n * col_chunk, (n + 1) * col_chunk)
        cb = slice(d + n * col_chunk, d + (n + 1) * col_chunk)
        for k in range(convs_per_chunk):
            conv_lane_block(n * convs_per_chunk + k)
        h = h_ref[...]
        gate_a = jax.nn.sigmoid(_dot(h, gate_weight(ca.start)) + bgate_ref[:, ca])
        ga_ref[:, ca] = gate_a * _dot(mixed_ref[...], pp_ref[:, ca])
        gb_ref[:, ca] = jax.nn.sigmoid(_dot(h, gate_weight(cb.start)) + bgate_ref[:, cb])

    for r0 in range(0, tm, ln_rows):
        y = [conv_ref[cb, r0:r0 + ln_rows, :] for cb in range(n_lane_blocks)]
        mu = jnp.sum(sum(y), axis=-1, keepdims=True) * (1.0 / conv_width)
        yc = [v - mu for v in y]
        var = jnp.sum(sum(v * v for v in yc), axis=-1, keepdims=True) * (1.0 / conv_width)
        inv = lax.rsqrt(var + EPS)
        for cb in range(n_lane_blocks):
            cs = slice(cb * LANES, (cb + 1) * LANES)
            z = (yc[cb] * inv) * lng_ref[:, cs] + lnb_ref[:, cs]
            c_ref[r0:r0 + ln_rows, cs] = (z * jax.nn.sigmoid(z)).astype(_BF16)

    c = c_ref[...]
    for n in range(n_chunks):
        ca = slice(n * col_chunk, (n + 1) * col_chunk)
        bb = _dot(c, cp_ref[:, ca])
        m_ref[:, ca] = (ga_ref[:, ca] + gb_ref[:, ca] * bb).astype(_BF16)
    merged = m_ref[...]
    for n in range(n_chunks):
        ca = slice(n * col_chunk, (n + 1) * col_chunk)
        o_ref[:, ca] = x_ref[:, ca] + _dot(merged, wo_ref[:, ca])


def _mix_out(x, g, w_in, gate_col, b_gate, mixed, glu, conv_dw, conv_b, ln_g, ln_b, pool_w_proj,
             conv_w_proj, w_out, *, seq, tm=256):
    m, d = x.shape
    pool_width = mixed.shape[1]
    n_taps, conv_width = conv_dw.shape
    assert n_taps - 1 <= HALO and seq % tm == 0 and tm % (SUBLANES * ROW_STRIDE) == 0
    halo_blocks = tm // HALO
    n_lane_blocks = conv_width // LANES
    col_chunk = 512
    gate_block = math.gcd(gate_col, d)
    n_gate_blocks = 2 * d // gate_block
    assert gate_block % col_chunk == 0 and gate_col + 2 * d == w_in.shape[1]
    assert d % col_chunk == 0 and n_lane_blocks == 2 * (d // col_chunk)

    return pl.pallas_call(
        functools.partial(_mix_out_kernel, tiles_per_seq=seq // tm, col_chunk=col_chunk, ln_rows=32),
        grid=(m // tm,),
        in_specs=[
            pl.BlockSpec((tm, d), lambda i: (i, 0)),
            _resident((1, d)),
            [pl.BlockSpec((d, gate_block), lambda i, k=k: (0, gate_col // gate_block + k),
                          pipeline_mode=pl.Buffered(1)) for k in range(n_gate_blocks)],
            _resident(b_gate.shape),
            pl.BlockSpec((tm, pool_width), lambda i: (i, 0)),
            pl.BlockSpec((tm, conv_width), lambda i: (i, 0)),
            pl.BlockSpec((HALO, conv_width), lambda i: (jnp.maximum(i * halo_blocks - 1, 0), 0)),
            _resident((n_taps, n_lane_blocks, LANES)),
            _resident((n_lane_blocks, LANES)),
            _resident((1, conv_width)),
            _resident((1, conv_width)),
            _resident(pool_w_proj.shape),
            _resident(conv_w_proj.shape),
            _resident(w_out.shape),
        ],
        out_specs=pl.BlockSpec((tm, d), lambda i: (i, 0)),
        out_shape=jax.ShapeDtypeStruct((m, d), _F32),
        scratch_shapes=[
            pltpu.VMEM((tm, d), _BF16),
            pltpu.VMEM((n_lane_blocks, HALO + tm, LANES), _F32),
            pltpu.VMEM((n_lane_blocks, tm, LANES), _F32),
            pltpu.VMEM((tm, conv_width), _BF16),
            pltpu.VMEM((tm, d), _F32),
            pltpu.VMEM((tm, d), _F32),
            pltpu.VMEM((tm, d), _BF16),
        ],
        compiler_params=pltpu.CompilerParams(
            dimension_semantics=("parallel",), vmem_limit_bytes=VMEM_LIMIT_BYTES),
        name="mix_out",
    )(x, g, [w_in] * n_gate_blocks, b_gate, mixed, glu, glu, conv_dw.reshape(n_taps, n_lane_blocks, LANES),
      conv_b.reshape(n_lane_blocks, LANES), ln_g, ln_b, pool_w_proj, conv_w_proj, w_out)


def kernel(x, ffn1_norm, ffn1_w_in, ffn1_w_out, mix_norm, w_in, b_in, pool_w_grp, pool_scale,
           pool_w_proj, conv_dw, conv_b, conv_ln_g, conv_ln_b, conv_w_proj, w_out, ffn2_norm,
           ffn2_w_in, ffn2_w_out, final_norm):
    batch, seq, d = x.shape
    depth = ffn1_norm.shape[0]
    if depth == 0:
        raise ValueError("depth must be positive")
    pool_width = pool_w_proj.shape[1]
    conv_width = conv_w_proj.shape[1]
    n_pc = pool_width + 2 * conv_width

    def row(v):
        return v.reshape(1, -1).astype(_F32)

    xf = x.reshape(batch * seq, d)
    for l in range(depth):
        xf = _ffn(xf, row(ffn1_norm[l]), ffn1_w_in[l], ffn1_w_out[l])
        w_in_bf16 = w_in[l].astype(_BF16)
        mixed, glu = _mix_in(
            xf, row(mix_norm[l]), w_in_bf16, row(b_in[l][:n_pc]),
            pool_w_grp[l].astype(_BF16), row(pool_scale[l]), seq=seq)
        xf = _mix_out(
            xf, row(mix_norm[l]), w_in_bf16, n_pc, row(b_in[l][n_pc:]), mixed, glu,
            conv_dw[l], conv_b[l], row(conv_ln_g[l]), row(conv_ln_b[l]),
            pool_w_proj[l].astype(_BF16), conv_w_proj[l].astype(_BF16), w_out[l].astype(_BF16),
            seq=seq)
        final_g = row(final_norm) if l == depth - 1 else None
        xf = _ffn(xf, row(ffn2_norm[l]), ffn2_w_in[l], ffn2_w_out[l], final_g)
    return xf.reshape(batch, seq, d)
```
